```python
import math
import jax, jax.numpy as jnp
from jax import lax
import numpy as np

D_MODEL = 1024
BATCH = 16
SEQ = 2048
DEPTH = 4

N_EVEN = (DEPTH + 1) // 2
N_ODD = DEPTH // 2
D_FF = 4 * D_MODEL
NORM_EPS = 1e-5

SSD_WIDTH = D_MODEL
SSD_HEAD_DIM = 64
SSD_HEADS = SSD_WIDTH // SSD_HEAD_DIM
SSD_GROUPS = 2
SSD_STATE = 128
SSD_CONV = 4
SSD_CHUNK = 128
SSD_XBC = SSD_WIDTH + 2 * SSD_GROUPS * SSD_STATE
SSD_IN = SSD_WIDTH + SSD_XBC + SSD_HEADS

GMLP_WIDTH = D_MODEL
GMLP_GROUPS = 8
GMLP_GROUP_DIM = GMLP_WIDTH // GMLP_GROUPS
GMLP_CHUNK = 128
EVEN_IN = SSD_IN + 2 * GMLP_WIDTH
EVEN_MIX = SSD_WIDTH + GMLP_WIDTH

CONF_WIDTH = D_MODEL
CONF_KERNEL = 31

HGRN_HEADS = 8
HGRN_EXPAND = 128
HGRN_HEAD_V = D_MODEL // HGRN_HEADS
HGRN_K = HGRN_HEADS * HGRN_EXPAND
HGRN_V = HGRN_HEADS * HGRN_HEAD_V
HGRN_CHUNK = 64
ODD_IN = 2 * CONF_WIDTH + 2 * HGRN_K + 2 * HGRN_V
ODD_MIX = CONF_WIDTH + HGRN_V

kernel_name = "hybrid_ssd_gmlp_conformer_hgrn2_trunk"


def rmsnorm(x, g):
    xf = x.astype(jnp.float32)
    y = xf * lax.rsqrt(jnp.mean(xf * xf, axis=-1, keepdims=True) + NORM_EPS)
    return (y * g).astype(x.dtype)


def layernorm(x, g, b):
    xf = x.astype(jnp.float32)
    mu = jnp.mean(xf, axis=-1, keepdims=True)
    var = jnp.mean(jnp.square(xf - mu), axis=-1, keepdims=True)
    return ((xf - mu) * lax.rsqrt(var + NORM_EPS) * g + b).astype(x.dtype)


def causal_depthwise_conv(x, w, b):
    width = w.shape[0]
    xp = jnp.pad(x, ((0, 0), (width - 1, 0), (0, 0)))
    y = lax.conv_general_dilated(xp, w[:, None, :].astype(x.dtype), window_strides=(1,), padding="VALID",
                                 dimension_numbers=("NWC", "WIO", "NWC"), feature_group_count=x.shape[-1])
    return y + b


def ssd_mixer(proj, conv_w, conv_b, dt_bias, a_log, d_skip, norm_g):
    bsz, seqlen, _ = proj.shape
    nc, q = seqlen // SSD_CHUNK, SSD_CHUNK
    g, hpg, p, n = SSD_GROUPS, SSD_HEADS // SSD_GROUPS, SSD_HEAD_DIM, SSD_STATE
    z, xbc, dt = jnp.split(proj, [SSD_WIDTH, SSD_WIDTH + SSD_XBC], axis=-1)
    xbc = jax.nn.silu(causal_depthwise_conv(xbc, conv_w, conv_b))
    xs, bm, cm = jnp.split(xbc, [SSD_WIDTH, SSD_WIDTH + g * n], axis=-1)
    xs = xs.astype(jnp.float32).reshape(bsz, nc, q, g, hpg, p)
    bm = bm.astype(jnp.float32).reshape(bsz, nc, q, g, n)
    cm = cm.astype(jnp.float32).reshape(bsz, nc, q, g, n)
    dt = jax.nn.softplus(dt.astype(jnp.float32).reshape(bsz, nc, q, g, hpg) + dt_bias.reshape(g, hpg))
    a = -jnp.exp(a_log.astype(jnp.float32)).reshape(g, hpg)
    acs = jnp.cumsum(dt * a, axis=2)
    xdt = xs * dt[..., None]
    causal = jnp.tril(jnp.ones((q, q), dtype=bool))
    seg = acs[:, :, :, None] - acs[:, :, None, :]
    lmat = jnp.exp(jnp.where(causal[:, :, None, None], seg, -jnp.inf))
    cb = jnp.einsum("bclgn,bcsgn->bclsg", cm, bm)
    y_diag = jnp.einsum("bclsgj,bcsgjp->bclgjp", cb[..., None] * lmat, xdt)
    decay_to_end = jnp.exp(acs[:, :, -1:] - acs)
    decay_from_start = jnp.exp(acs)
    chunk_decay = jnp.exp(acs[:, :, -1])

    def step(h, inp):
        b_c, c_c, x_c, dte, dfs, cd = inp
        y_off = jnp.einsum("blgn,bgjpn,blgj->blgjp", c_c, h, dfs)
        h = h * cd[..., None, None] + jnp.einsum("blgn,blgj,blgjp->bgjpn", b_c, dte, x_c)
        return h, y_off

    h0 = jnp.zeros((bsz, g, hpg, p, n), jnp.float32)
    sw = lambda t: jnp.moveaxis(t, 1, 0)
    _, y_off = lax.scan(step, h0, (sw(bm), sw(cm), sw(xdt), sw(decay_to_end), sw(decay_from_start), sw(chunk_decay)))
    y = y_diag + jnp.moveaxis(y_off, 0, 1) + d_skip.astype(jnp.float32).reshape(g, hpg)[:, :, None] * xs
    y = y.reshape(bsz, seqlen, SSD_WIDTH) * jax.nn.silu(z.astype(jnp.float32))
    yg = y.reshape(bsz, seqlen, g, SSD_WIDTH // g)
    yg = yg * lax.rsqrt(jnp.mean(yg * yg, axis=-1, keepdims=True) + NORM_EPS)
    return (yg.reshape(bsz, seqlen, SSD_WIDTH) * norm_g).astype(proj.dtype)


def gmlp_mixer(proj, ln_g, ln_b, w_s, b_s):
    bsz, seqlen, _ = proj.shape
    nc, q = seqlen // GMLP_CHUNK, GMLP_CHUNK
    u, v = jnp.split(jax.nn.gelu(proj, approximate=False), 2, axis=-1)
    v = layernorm(v, ln_g, ln_b).reshape(bsz, nc, q, GMLP_GROUPS, GMLP_GROUP_DIM)
    causal = jnp.tril(jnp.ones((q, q), dtype=bool))
    w = jnp.where(causal[None], w_s, 0.0)
    mixed = jnp.einsum("gts,bcsgd->bctgd", w, v) + jnp.transpose(b_s)[:, :, None]
    return (u * mixed.reshape(bsz, seqlen, GMLP_WIDTH)).astype(proj.dtype)


def conformer_conv_mixer(proj, conv_w, conv_b, ln_g, ln_b):
    a, gate = jnp.split(proj, 2, axis=-1)
    h = a * jax.nn.sigmoid(gate)
    h = causal_depthwise_conv(h, conv_w, conv_b)
    return jax.nn.silu(layernorm(h, ln_g, ln_b)).astype(proj.dtype)


def hgrn2_mixer(proj, lower_bound, norm_g):
    bsz, seqlen, _ = proj.shape
    nc, c, h, dk, dv = seqlen // HGRN_CHUNK, HGRN_CHUNK, HGRN_HEADS, HGRN_EXPAND, HGRN_HEAD_V
    qx, fx, ix, gx = jnp.split(proj, [HGRN_K, 2 * HGRN_K, 2 * HGRN_K + HGRN_V], axis=-1)
    fx = fx.astype(jnp.float32).reshape(bsz, nc, c, h, dk)
    lb = lower_bound.reshape(h, dk)
    log_f = jnp.logaddexp(jnp.log(lb), jnp.log1p(-lb) + jax.nn.log_sigmoid(fx))
    k = (1.0 - lb) * jax.nn.sigmoid(-fx)
    qv = qx.astype(jnp.float32).reshape(bsz, nc, c, h, dk) * (dk ** -0.5)
    iv = ix.astype(jnp.float32).reshape(bsz, nc, c, h, dv)
    bcum = jnp.cumsum(log_f, axis=2)
    mid = bcum[:, :, c // 2:c // 2 + 1]
    qg = qv * jnp.exp(bcum - mid)
    kg = k * jnp.exp(mid - bcum)
    causal = jnp.tril(jnp.ones((c, c), dtype=bool))
    att = jnp.where(causal, jnp.einsum("bnthk,bnshk->bnhts", qg, kg), 0.0)
    o_intra = jnp.einsum("bnhts,bnshv->bnthv", att, iv)
    q_dec = qv * jnp.exp(bcum)
    k_dec = k * jnp.exp(bcum[:, :, -1:] - bcum)
    chunk_decay = jnp.exp(bcum[:, :, -1])

    def step(s, inp):
        qd, kd, vc, cd = inp
        o = jnp.einsum("blhk,bhkv->blhv", qd, s)
        s = s * cd[..., None] + jnp.einsum("blhk,blhv->bhkv", kd, vc)
        return s, o

    s0 = jnp.zeros((bsz, h, dk, dv), jnp.float32)
    sw = lambda t: jnp.moveaxis(t, 1, 0)
    _, o_inter = lax.scan(step, s0, (sw(q_dec), sw(k_dec), sw(iv), sw(chunk_decay)))
    o = (o_intra + jnp.moveaxis(o_inter, 0, 1)).reshape(bsz, seqlen, h, dv)
    o = o * lax.rsqrt(jnp.mean(o * o, axis=-1, keepdims=True) + NORM_EPS) * norm_g.reshape(h, dv)
    o = o.reshape(bsz, seqlen, HGRN_V) * jax.nn.silu(gx.astype(jnp.float32))
    return o.astype(proj.dtype)


def setup_inputs(seed: int = 0) -> dict:
    key = jax.random.key(seed)
    ks = iter(jax.random.split(key, 32))
    nrm = lambda shape, scale: jax.random.normal(next(ks), shape, jnp.float32) * scale
    gain = lambda shape: 1.0 + nrm(shape, 0.02)
    dt0 = jnp.exp(jax.random.uniform(next(ks), (N_EVEN, SSD_HEADS), jnp.float32, math.log(1e-3), math.log(1e-1)))
    return {
        "x": nrm((BATCH, SEQ, D_MODEL), 1.0),
        "even_w_in": nrm((N_EVEN, D_MODEL, EVEN_IN), D_MODEL ** -0.5),
        "even_w_out": nrm((N_EVEN, EVEN_MIX, D_MODEL), EVEN_MIX ** -0.5),
        "ssd_conv_w": nrm((N_EVEN, SSD_CONV, SSD_XBC), SSD_CONV ** -0.5),
        "ssd_conv_b": nrm((N_EVEN, SSD_XBC), 0.02),
        "ssd_dt_bias": dt0 + jnp.log(-jnp.expm1(-dt0)),
        "ssd_a_log": jnp.log(jax.random.uniform(next(ks), (N_EVEN, SSD_HEADS), jnp.float32, 1.0, 16.0)),
        "ssd_d": gain((N_EVEN, SSD_HEADS)),
        "ssd_norm_g": gain((N_EVEN, SSD_WIDTH)),
        "gmlp_ln_g": gain((N_EVEN, GMLP_WIDTH)),
        "gmlp_ln_b": nrm((N_EVEN, GMLP_WIDTH), 0.02),
        "gmlp_w_s": nrm((N_EVEN, GMLP_GROUPS, GMLP_CHUNK, GMLP_CHUNK), GMLP_CHUNK ** -0.5),
        "gmlp_b_s": gain((N_EVEN, GMLP_GROUPS, GMLP_CHUNK)),
        "odd_w_in": nrm((N_ODD, D_MODEL, ODD_IN), D_MODEL ** -0.5),
        "odd_w_out": nrm((N_ODD, ODD_MIX, D_MODEL), ODD_MIX ** -0.5),
        "conf_conv_w": nrm((N_ODD, CONF_KERNEL, CONF_WIDTH), CONF_KERNEL ** -0.5),
        "conf_conv_b": nrm((N_ODD, CONF_WIDTH), 0.02),
        "conf_ln_g": gain((N_ODD, CONF_WIDTH)),
        "conf_ln_b": nrm((N_ODD, CONF_WIDTH), 0.02),
        "hgrn_lb_logits": nrm((N_ODD, HGRN_K), 0.1),
        "hgrn_norm_g": gain((N_ODD, HGRN_V)),
        "mix_norm_g": gain((DEPTH, D_MODEL)),
        "ffn_norm_g": gain((DEPTH, D_MODEL)),
        "ffn_w1": nrm((DEPTH, D_MODEL, D_FF), D_MODEL ** -0.5),
        "ffn_w2": nrm((DEPTH, D_FF, D_MODEL), D_FF ** -0.5),
        "final_norm_g": gain((D_MODEL,)),
    }


def reference(x, even_w_in, even_w_out, ssd_conv_w, ssd_conv_b, ssd_dt_bias, ssd_a_log, ssd_d, ssd_norm_g,
              gmlp_ln_g, gmlp_ln_b, gmlp_w_s, gmlp_b_s, odd_w_in, odd_w_out, conf_conv_w, conf_conv_b,
              conf_ln_g, conf_ln_b, hgrn_lb_logits, hgrn_norm_g, mix_norm_g, ffn_norm_g, ffn_w1, ffn_w2,
              final_norm_g):
    lb_all = jnp.cumsum(jax.nn.softmax(hgrn_lb_logits.astype(jnp.float32), axis=0), axis=0)
    lb_all = lb_all - lb_all[:1]
    for layer in range(DEPTH):
        h = rmsnorm(x, mix_norm_g[layer])
        if layer % 2 == 0:
            e = layer // 2
            proj = h @ even_w_in[e]
            ssd_y = ssd_mixer(proj[..., :SSD_IN], ssd_conv_w[e], ssd_conv_b[e], ssd_dt_bias[e],
                              ssd_a_log[e], ssd_d[e], ssd_norm_g[e])
            gmlp_y = gmlp_mixer(proj[..., SSD_IN:], gmlp_ln_g[e], gmlp_ln_b[e], gmlp_w_s[e], gmlp_b_s[e])
            y = jnp.concatenate([ssd_y, gmlp_y], axis=-1) @ even_w_out[e]
        else:
            o = layer // 2
            proj = h @ odd_w_in[o]
            conf_y = conformer_conv_mixer(proj[..., :2 * CONF_WIDTH], conf_conv_w[o], conf_conv_b[o],
                                          conf_ln_g[o], conf_ln_b[o])
            hgrn_y = hgrn2_mixer(proj[..., 2 * CONF_WIDTH:], lb_all[o], hgrn_norm_g[o])
            y = jnp.concatenate([conf_y, hgrn_y], axis=-1) @ odd_w_out[o]
        x = x + y
        h = rmsnorm(x, ffn_norm_g[layer])
        x = x + jnp.square(jax.nn.relu(h @ ffn_w1[layer])) @ ffn_w2[layer]
    return rmsnorm(x, final_norm_g)
```

```python
import functools
import math

import jax
import jax.numpy as jnp
import numpy as np
from jax import lax
from jax.experimental import pallas as pl
from jax.experimental.pallas import tpu as pltpu

F32 = jnp.float32
BF16 = jnp.bfloat16

D_MODEL = 1024
D_FF = 4 * D_MODEL
NORM_EPS = 1e-5

SSD_WIDTH = D_MODEL
SSD_HEAD_DIM = 64
SSD_HEADS = SSD_WIDTH // SSD_HEAD_DIM
SSD_GROUPS = 2
SSD_HPG = SSD_HEADS // SSD_GROUPS
SSD_STATE = 128
SSD_CONV = 4
SSD_CHUNK = 128
SSD_XBC = SSD_WIDTH + 2 * SSD_GROUPS * SSD_STATE
GROUP_W = SSD_HPG * SSD_HEAD_DIM

GMLP_WIDTH = D_MODEL
GMLP_GROUPS = 8
GMLP_GROUP_DIM = GMLP_WIDTH // GMLP_GROUPS
GMLP_CHUNK = 128

CONF_WIDTH = D_MODEL
CONF_KERNEL = 31
CONF_HIST = 32

HGRN_HEADS = 8
HGRN_EXPAND = 128
HGRN_HEAD_V = D_MODEL // HGRN_HEADS
HGRN_K = HGRN_HEADS * HGRN_EXPAND
HGRN_V = HGRN_HEADS * HGRN_HEAD_V
HGRN_CHUNK = 64

LANES = 128
SUBLANES = 8

E_Z = 0
E_XBC = E_Z + SSD_WIDTH
E_U = E_XBC + SSD_XBC
E_V = E_U + GMLP_WIDTH
E_DT = E_V + GMLP_WIDTH
E_NIN = E_DT + LANES
EVEN_MIX = SSD_WIDTH + GMLP_WIDTH

O_A = 0
O_GATE = O_A + CONF_WIDTH
O_Q = O_GATE + CONF_WIDTH
O_F = O_Q + HGRN_K
O_I = O_F + HGRN_K
O_G = O_I + HGRN_V
O_NIN = O_G + HGRN_V
ODD_MIX = CONF_WIDTH + HGRN_V

SEQ_BLOCK = 256
FFN_ROWS = 512
PROJ_COLS = 512
VMEM_LIMIT = 56 * 1024 * 1024


def _dot(a, b):
    return jnp.dot(a, b, preferred_element_type=F32)


def _dot_nt(a, b):
    return lax.dot_general(a, b, (((1,), (1,)), ((), ())), preferred_element_type=F32)


def _dot_tn(a, b):
    return lax.dot_general(a, b, (((0,), (0,)), ((), ())), preferred_element_type=F32)


def _rmsnorm(x, g):
    return x * lax.rsqrt(jnp.mean(x * x, axis=-1, keepdims=True) + NORM_EPS) * g


def _silu(x):
    return x * jax.nn.sigmoid(x)


def _gelu(x):
    return 0.5 * x * (1.0 + lax.erf(x * (2.0 ** -0.5)))


def _split_bf16(x, parts):
    out = []
    rem = x
    for _ in range(parts):
        p = rem.astype(BF16)
        out.append(p)
        rem = rem - p.astype(F32)
    return out


def _project(h, w_ref, proj_ref, row0, rows, n_cols):
    for c0 in range(0, n_cols, PROJ_COLS):
        c1 = min(c0 + PROJ_COLS, n_cols)
        proj_ref[row0:row0 + rows, c0:c1] = _dot(h, w_ref[:, c0:c1])


def _ssd_chunk(r0, m0, proj_ref, xbc_ref, y_ref, state_ref, mix_ref, convw_ref, convb_ref, dtb_ref,
               a_ref, dskip_ref, sng_ref, expand_ref, tril_ref):
    q = SSD_CHUNK
    for c0 in range(0, SSD_XBC, 2 * LANES):
        cols = slice(c0, c0 + 2 * LANES)
        pcols = slice(E_XBC + c0, E_XBC + c0 + 2 * LANES)
        acc = jnp.broadcast_to(convb_ref[:, cols], (q, 2 * LANES))
        for k in range(SSD_CONV):
            acc = acc + proj_ref[pl.ds(r0 - (SSD_CONV - 1) + k, q), pcols] * convw_ref[k:k + 1, cols]
        xbc_ref[:, cols] = _silu(acc)

    dt = jax.nn.softplus(proj_ref[r0:r0 + q, E_DT:E_DT + LANES] + dtb_ref[...])
    da = dt * a_ref[...]
    cs = _dot(tril_ref[...], jnp.concatenate(_split_bf16(da, 3), axis=1))
    acs = cs[:, :LANES] + cs[:, LANES:2 * LANES] + cs[:, 2 * LANES:]
    acs_t = acs.T

    def expand(v):
        return _dot(jnp.concatenate(_split_bf16(v, 2), axis=1), expand_ref[...])

    dt_x = expand(dt)
    acs_x = expand(acs)

    row = lax.broadcasted_iota(jnp.int32, (q, q), 0)
    col = lax.broadcasted_iota(jnp.int32, (q, q), 1)
    causal = row >= col
    lane = lax.broadcasted_iota(jnp.int32, (q, LANES), 1)
    first_half = lane < SSD_HEAD_DIM

    for g in range(SSD_GROUPS):
        gc = slice(g * GROUP_W, (g + 1) * GROUP_W)
        acs_g = acs_x[:, gc]
        last_g = acs_g[q - 1:q, :]
        xs_g = xbc_ref[:, gc]
        xdt_g = xs_g * dt_x[:, gc]
        xdt_b = xdt_g.astype(BF16)
        xw_b = (xdt_g * jnp.exp(last_g - acs_g)).astype(BF16)
        bm_b = xbc_ref[:, SSD_WIDTH + g * SSD_STATE:SSD_WIDTH + (g + 1) * SSD_STATE].astype(BF16)
        cm0 = SSD_WIDTH + SSD_GROUPS * SSD_STATE + g * SSD_STATE
        cm_b = xbc_ref[:, cm0:cm0 + SSD_STATE].astype(BF16)
        st = state_ref[g]
        y_off = _dot(cm_b, st.astype(BF16)) * jnp.exp(acs_g)
        state_ref[g] = st * jnp.exp(last_g) + _dot_tn(bm_b, xw_b)
        cb = _dot_nt(cm_b, bm_b)
        for jp in range(SSD_HPG // 2):
            pc = slice(jp * LANES, (jp + 1) * LANES)
            x_pair = xdt_b[:, pc]
            yd = []
            for jj in range(2):
                hd = g * SSD_HPG + 2 * jp + jj
                seg = acs[:, hd:hd + 1] - acs_t[hd:hd + 1, :]
                m = (cb * jnp.where(causal, jnp.exp(seg), 0.0)).astype(BF16)
                yd.append(_dot(m, x_pair))
            y_diag = jnp.where(first_half, yd[0], yd[1])
            yc = slice(g * GROUP_W + jp * LANES, g * GROUP_W + (jp + 1) * LANES)
            y_ref[:, yc] = y_diag + y_off[:, pc] + dskip_ref[:, yc] * xs_g[:, pc]

    for g in range(SSD_GROUPS):
        gc = slice(g * GROUP_W, (g + 1) * GROUP_W)
        yg = y_ref[:, gc] * _silu(proj_ref[r0:r0 + q, E_Z + g * GROUP_W:E_Z + (g + 1) * GROUP_W])
        yg = yg * lax.rsqrt(jnp.mean(yg * yg, axis=-1, keepdims=True) + NORM_EPS)
        mix_ref[m0:m0 + q, gc] = (yg * sng_ref[:, gc]).astype(BF16)


def _gmlp_chunk(r0, m0, proj_ref, u_ref, v_ref, mix_ref, lng_ref, lnb_ref, wsp_ref, bsp_ref):
    q = GMLP_CHUNK
    tot = jnp.zeros((q, 1), F32)
    for c0 in range(0, GMLP_WIDTH, 2 * LANES):
        cols = slice(c0, c0 + 2 * LANES)
        u_ref[:, cols] = _gelu(proj_ref[r0:r0 + q, E_U + c0:E_U + c0 + 2 * LANES])
        v = _gelu(proj_ref[r0:r0 + q, E_V + c0:E_V + c0 + 2 * LANES])
        v_ref[:, cols] = v
        tot = tot + jnp.sum(v, axis=-1, keepdims=True)
    mu = tot * (1.0 / GMLP_WIDTH)
    sq = jnp.zeros((q, 1), F32)
    for c0 in range(0, GMLP_WIDTH, 2 * LANES):
        dv = v_ref[:, c0:c0 + 2 * LANES] - mu
        sq = sq + jnp.sum(dv * dv, axis=-1, keepdims=True)
    inv = lax.rsqrt(sq * (1.0 / GMLP_WIDTH) + NORM_EPS)
    for g in range(GMLP_GROUPS):
        cols = slice(g * GMLP_GROUP_DIM, (g + 1) * GMLP_GROUP_DIM)
        vn = ((v_ref[:, cols] - mu) * inv * lng_ref[:, cols] + lnb_ref[:, cols]).astype(BF16)
        mixed = _dot(wsp_ref[g], vn) + bsp_ref[:, cols]
        mix_ref[m0:m0 + q, SSD_WIDTH + g * GMLP_GROUP_DIM:SSD_WIDTH + (g + 1) * GMLP_GROUP_DIM] = (
            u_ref[:, cols] * mixed).astype(BF16)


def _even_kernel(x_ref, ng_ref, win_ref, wout_ref, convw_ref, convb_ref, dtb_ref, a_ref, dskip_ref,
                 sng_ref, lng_ref, lnb_ref, wsp_ref, bsp_ref, expand_ref, tril_ref, o_ref,
                 proj_ref, mix_ref, xbc_ref, y_ref, u_ref, v_ref, state_ref):
    ts = x_ref.shape[1]
    pad = SUBLANES
    xbc_cols = slice(E_XBC, E_XBC + SSD_XBC)

    @pl.when(pl.program_id(1) == 0)
    def _():
        state_ref[...] = jnp.zeros(state_ref.shape, F32)
        proj_ref[0:pad, xbc_cols] = jnp.zeros((pad, SSD_XBC), F32)

    @pl.when(pl.program_id(1) > 0)
    def _():
        proj_ref[0:pad, xbc_cols] = proj_ref[ts:ts + pad, xbc_cols]

    xb = x_ref[0]
    h = _rmsnorm(xb, ng_ref[...]).astype(BF16)
    _project(h, win_ref, proj_ref, pad, ts, E_NIN)

    for c in range(ts // SSD_CHUNK):
        _ssd_chunk(pad + c * SSD_CHUNK, c * SSD_CHUNK, proj_ref, xbc_ref, y_ref, state_ref, mix_ref,
                   convw_ref, convb_ref, dtb_ref, a_ref, dskip_ref, sng_ref, expand_ref, tril_ref)
        _gmlp_chunk(pad + c * GMLP_CHUNK, c * GMLP_CHUNK, proj_ref, u_ref, v_ref, mix_ref,
                    lng_ref, lnb_ref, wsp_ref, bsp_ref)

    o_ref[0] = xb + _dot(mix_ref[...], wout_ref[...])


def _conf_chunk(r0, m0, proj_ref, glu_ref, cv_ref, mix_ref, convw_ref, convb_ref, lng_ref, lnb_ref):
    q = SSD_CHUNK
    g0 = CONF_HIST + m0
    for c0 in range(0, CONF_WIDTH, 2 * LANES):
        a = proj_ref[r0:r0 + q, O_A + c0:O_A + c0 + 2 * LANES]
        gate = proj_ref[r0:r0 + q, O_GATE + c0:O_GATE + c0 + 2 * LANES]
        glu_ref[g0:g0 + q, c0:c0 + 2 * LANES] = a * jax.nn.sigmoid(gate)
    tot = jnp.zeros((q, 1), F32)
    for c0 in range(0, CONF_WIDTH, 2 * LANES):
        cols = slice(c0, c0 + 2 * LANES)
        acc = jnp.broadcast_to(convb_ref[:, cols], (q, 2 * LANES))
        for k in range(CONF_KERNEL):
            acc = acc + glu_ref[pl.ds(g0 - (CONF_KERNEL - 1) + k, q), cols] * convw_ref[k:k + 1, cols]
        cv_ref[:, cols] = acc
        tot = tot + jnp.sum(acc, axis=-1, keepdims=True)
    mu = tot * (1.0 / CONF_WIDTH)
    sq = jnp.zeros((q, 1), F32)
    for c0 in range(0, CONF_WIDTH, 2 * LANES):
        dv = cv_ref[:, c0:c0 + 2 * LANES] - mu
        sq = sq + jnp.sum(dv * dv, axis=-1, keepdims=True)
    inv = lax.rsqrt(sq * (1.0 / CONF_WIDTH) + NORM_EPS)
    for c0 in range(0, CONF_WIDTH, 2 * LANES):
        cols = slice(c0, c0 + 2 * LANES)
        hn = (cv_ref[:, cols] - mu) * inv * lng_ref[:, cols] + lnb_ref[:, cols]
        mix_ref[m0:m0 + q, cols] = _silu(hn).astype(BF16)


def _hgrn_chunk(r0, m0, proj_ref, mix_ref, state_ref, lb_ref, hng_ref, tril_ref):
    c = HGRN_CHUNK
    row = lax.broadcasted_iota(jnp.int32, (c, c), 0)
    col = lax.broadcasted_iota(jnp.int32, (c, c), 1)
    causal = row >= col
    for hd in range(HGRN_HEADS):
        kc = slice(hd * HGRN_EXPAND, (hd + 1) * HGRN_EXPAND)
        vc = slice(hd * HGRN_HEAD_V, (hd + 1) * HGRN_HEAD_V)
        lb = lb_ref[:, kc]
        fx = proj_ref[r0:r0 + c, O_F + hd * HGRN_EXPAND:O_F + (hd + 1) * HGRN_EXPAND]
        log_sig = jnp.minimum(fx, 0.0) - jnp.log1p(jnp.exp(-jnp.abs(fx)))
        p = jnp.log(lb)
        r = jnp.log1p(-lb) + log_sig
        log_f = jnp.maximum(p, r) + jnp.log1p(jnp.exp(-jnp.abs(p - r)))
        kk = (1.0 - lb) * jax.nn.sigmoid(-fx)
        qv = proj_ref[r0:r0 + c, O_Q + hd * HGRN_EXPAND:O_Q + (hd + 1) * HGRN_EXPAND] * (HGRN_EXPAND ** -0.5)
        iv_b = proj_ref[r0:r0 + c, O_I + hd * HGRN_HEAD_V:O_I + (hd + 1) * HGRN_HEAD_V].astype(BF16)
        cs = _dot(tril_ref[...], jnp.concatenate(_split_bf16(log_f, 3), axis=1))
        bcum = cs[:, :LANES] + cs[:, LANES:2 * LANES] + cs[:, 2 * LANES:]
        mid = bcum[c // 2:c // 2 + 1, :]
        last = bcum[c - 1:c, :]
        qg = (qv * jnp.exp(bcum - mid)).astype(BF16)
        kg = (kk * jnp.exp(mid - bcum)).astype(BF16)
        att = jnp.where(causal, _dot_nt(qg, kg), 0.0).astype(BF16)
        o = _dot(att, iv_b)
        q_dec = (qv * jnp.exp(bcum)).astype(BF16)
        k_dec = (kk * jnp.exp(last - bcum)).astype(BF16)
        st = state_ref[hd]
        o = o + _dot_nt(q_dec, st.astype(BF16))
        state_ref[hd] = st * jnp.exp(last) + _dot_tn(iv_b, k_dec)
        o = o * lax.rsqrt(jnp.mean(o * o, axis=-1, keepdims=True) + NORM_EPS) * hng_ref[:, vc]
        gx = proj_ref[r0:r0 + c, O_G + hd * HGRN_HEAD_V:O_G + (hd + 1) * HGRN_HEAD_V]
        mix_ref[m0:m0 + c, CONF_WIDTH + hd * HGRN_HEAD_V:CONF_WIDTH + (hd + 1) * HGRN_HEAD_V] = (
            o * _silu(gx)).astype(BF16)


def _odd_kernel(x_ref, ng_ref, win_ref, wout_ref, convw_ref, convb_ref, lng_ref, lnb_ref, lb_ref,
                hng_ref, tril_ref, o_ref, proj_ref, mix_ref, glu_ref, cv_ref, state_ref):
    ts = x_ref.shape[1]

    @pl.when(pl.program_id(1) == 0)
    def _():
        state_ref[...] = jnp.zeros(state_ref.shape, F32)
        glu_ref[0:CONF_HIST, :] = jnp.zeros((CONF_HIST, CONF_WIDTH), F32)

    @pl.when(pl.program_id(1) > 0)
    def _():
        glu_ref[0:CONF_HIST, :] = glu_ref[ts:ts + CONF_HIST, :]

    xb = x_ref[0]
    h = _rmsnorm(xb, ng_ref[...]).astype(BF16)
    _project(h, win_ref, proj_ref, 0, ts, O_NIN)

    for c in range(ts // SSD_CHUNK):
        _conf_chunk(c * SSD_CHUNK, c * SSD_CHUNK, proj_ref, glu_ref, cv_ref, mix_ref,
                    convw_ref, convb_ref, lng_ref, lnb_ref)
    for c in range(ts // HGRN_CHUNK):
        _hgrn_chunk(c * HGRN_CHUNK, c * HGRN_CHUNK, proj_ref, mix_ref, state_ref, lb_ref, hng_ref, tril_ref)

    o_ref[0] = xb + _dot(mix_ref[...], wout_ref[...])


def _ffn_kernel(x_ref, ng_ref, w1_ref, w2_ref, fg_ref, o_ref, hid_ref, *, final_norm):
    xb = x_ref[...]
    h = _rmsnorm(xb, ng_ref[...]).astype(BF16)
    for c0 in range(0, D_FF, PROJ_COLS):
        a = jnp.maximum(_dot(h, w1_ref[:, c0:c0 + PROJ_COLS]), 0.0)
        hid_ref[:, c0:c0 + PROJ_COLS] = (a * a).astype(BF16)
    y = xb + _dot(hid_ref[...], w2_ref[...])
    if final_norm:
        y = _rmsnorm(y, fg_ref[...])
    o_ref[...] = y


def _const_spec(shape):
    zeros = (0,) * len(shape)
    return pl.BlockSpec(shape, lambda *_: zeros)


def _mixer_call(body, x, consts, scratch, name):
    bsz, seq, d = x.shape
    ts = SEQ_BLOCK
    x_spec = pl.BlockSpec((1, ts, d), lambda b, s: (b, s, 0))
    return pl.pallas_call(
        body,
        grid=(bsz, seq // ts),
        in_specs=[x_spec] + [_const_spec(c.shape) for c in consts],
        out_specs=x_spec,
        out_shape=jax.ShapeDtypeStruct(x.shape, x.dtype),
        scratch_shapes=scratch,
        compiler_params=pltpu.CompilerParams(dimension_semantics=("arbitrary", "arbitrary"),
                                             vmem_limit_bytes=VMEM_LIMIT),
        name=name,
    )(x, *consts)


def _row(v):
    return v.reshape(1, -1).astype(F32)


def _lane_pad(v):
    return jnp.pad(_row(v), ((0, 0), (0, LANES - v.shape[-1])))


def _even_layer(x, ng, w_in, w_out, conv_w, conv_b, dt_bias, a_log, d_skip, ssd_ng, ln_g, ln_b, w_s, b_s):
    dt0 = SSD_WIDTH + SSD_XBC
    w_in_r = jnp.concatenate(
        [w_in[:, :dt0], w_in[:, dt0 + SSD_HEADS:], w_in[:, dt0:dt0 + SSD_HEADS],
         jnp.zeros((D_MODEL, LANES - SSD_HEADS), w_in.dtype)], axis=1).astype(BF16)
    tri = np.tril(np.ones((GMLP_CHUNK, GMLP_CHUNK), np.float32))
    w_sp = jnp.where(tri[None] > 0, w_s, 0.0).astype(BF16)
    b_sp = jnp.repeat(jnp.transpose(b_s).astype(F32), GMLP_GROUP_DIM, axis=1)
    expand = np.zeros((2 * LANES, SSD_WIDTH), np.float32)
    for part in range(2):
        for hd in range(SSD_HEADS):
            expand[part * LANES + hd, hd * SSD_HEAD_DIM:(hd + 1) * SSD_HEAD_DIM] = 1.0
    consts = [
        _row(ng), w_in_r, w_out.astype(BF16), conv_w.astype(F32), _row(conv_b), _lane_pad(dt_bias),
        _lane_pad(-jnp.exp(a_log.astype(F32))), _row(jnp.repeat(d_skip.astype(F32), SSD_HEAD_DIM)),
        _row(ssd_ng), _row(ln_g), _row(ln_b), w_sp, b_sp,
        jnp.asarray(expand, BF16), jnp.asarray(tri, BF16),
    ]
    ts = SEQ_BLOCK
    scratch = [
        pltpu.VMEM((SUBLANES + ts, E_NIN), F32),
        pltpu.VMEM((ts, EVEN_MIX), BF16),
        pltpu.VMEM((SSD_CHUNK, SSD_XBC), F32),
        pltpu.VMEM((SSD_CHUNK, SSD_WIDTH), F32),
        pltpu.VMEM((GMLP_CHUNK, GMLP_WIDTH), F32),
        pltpu.VMEM((GMLP_CHUNK, GMLP_WIDTH), F32),
        pltpu.VMEM((SSD_GROUPS, SSD_STATE, GROUP_W), F32),
    ]
    return _mixer_call(_even_kernel, x, consts, scratch, "even_mixer")


def _odd_layer(x, ng, w_in, w_out, conv_w, conv_b, ln_g, ln_b, lb, hgrn_ng):
    tri = np.tril(np.ones((HGRN_CHUNK, HGRN_CHUNK), np.float32))
    consts = [
        _row(ng), w_in.astype(BF16), w_out.astype(BF16), conv_w.astype(F32), _row(conv_b),
        _row(ln_g), _row(ln_b), _row(lb), _row(hgrn_ng), jnp.asarray(tri, BF16),
    ]
    ts = SEQ_BLOCK
    scratch = [
        pltpu.VMEM((ts, O_NIN), F32),
        pltpu.VMEM((ts, ODD_MIX), BF16),
        pltpu.VMEM((CONF_HIST + ts, CONF_WIDTH), F32),
        pltpu.VMEM((SSD_CHUNK, CONF_WIDTH), F32),
        pltpu.VMEM((HGRN_HEADS, HGRN_HEAD_V, HGRN_EXPAND), F32),
    ]
    return _mixer_call(_odd_kernel, x, consts, scratch, "odd_mixer")


def _ffn_layer(x2d, ng, w1, w2, final_g, final_norm):
    rows, d = x2d.shape
    tm = FFN_ROWS
    x_spec = pl.BlockSpec((tm, d), lambda i: (i, 0))
    consts = [_row(ng), w1.astype(BF16), w2.astype(BF16), _row(final_g)]
    return pl.pallas_call(
        functools.partial(_ffn_kernel, final_norm=final_norm),
        grid=(rows // tm,),
        in_specs=[x_spec] + [_const_spec(c.shape) for c in consts],
        out_specs=x_spec,
        out_shape=jax.ShapeDtypeStruct(x2d.shape, x2d.dtype),
        scratch_shapes=[pltpu.VMEM((tm, D_FF), BF16)],
        compiler_params=pltpu.CompilerParams(dimension_semantics=("arbitrary",),
                                             vmem_limit_bytes=VMEM_LIMIT),
        name="ffn",
    )(x2d, *consts)


def kernel(x, even_w_in, even_w_out, ssd_conv_w, ssd_conv_b, ssd_dt_bias, ssd_a_log, ssd_d, ssd_norm_g, gmlp_ln_g, gmlp_ln_b, gmlp_w_s, gmlp_b_s, odd_w_in, odd_w_out, conf_conv_w, conf_conv_b, conf_ln_g, conf_ln_b, hgrn_lb_logits, hgrn_norm_g, mix_norm_g, ffn_norm_g, ffn_w1, ffn_w2, final_norm_g):
    bsz, seq, d = x.shape
    depth = mix_norm_g.shape[0]
    assert d == D_MODEL and seq % SEQ_BLOCK == 0 and (bsz * seq) % FFN_ROWS == 0
    lb_all = jnp.cumsum(jax.nn.softmax(hgrn_lb_logits.astype(F32), axis=0), axis=0)
    lb_all = lb_all - lb_all[:1]
    for layer in range(depth):
        i = layer // 2
        if layer % 2 == 0:
            x = _even_layer(x, mix_norm_g[layer], even_w_in[i], even_w_out[i], ssd_conv_w[i], ssd_conv_b[i],
                            ssd_dt_bias[i], ssd_a_log[i], ssd_d[i], ssd_norm_g[i], gmlp_ln_g[i],
                            gmlp_ln_b[i], gmlp_w_s[i], gmlp_b_s[i])
        else:
            x = _odd_layer(x, mix_norm_g[layer], odd_w_in[i], odd_w_out[i], conf_conv_w[i], conf_conv_b[i],
                           conf_ln_g[i], conf_ln_b[i], lb_all[i], hgrn_norm_g[i])
        x = _ffn_layer(x.reshape(bsz * seq, d), ffn_norm_g[layer], ffn_w1[layer], ffn_w2[layer],
                       final_norm_g, layer == depth - 1).reshape(bsz, seq, d)
    return x
```

```python
import functools

import jax
import jax.numpy as jnp
import numpy as np
from jax import lax
from jax.experimental import pallas as pl
from jax.experimental.pallas import tpu as pltpu

F32 = jnp.float32
BF16 = jnp.bfloat16

D_MODEL = 1024
D_FF = 4 * D_MODEL
NORM_EPS = 1e-5

SSD_WIDTH = D_MODEL
SSD_HEAD_DIM = 64
SSD_HEADS = SSD_WIDTH // SSD_HEAD_DIM
SSD_GROUPS = 2
SSD_HPG = SSD_HEADS // SSD_GROUPS
SSD_STATE = 128
SSD_CONV = 4
SSD_CHUNK = 128
SSD_XBC = SSD_WIDTH + 2 * SSD_GROUPS * SSD_STATE
GROUP_W = SSD_HPG * SSD_HEAD_DIM

GMLP_WIDTH = D_MODEL
GMLP_GROUPS = 8
GMLP_GROUP_DIM = GMLP_WIDTH // GMLP_GROUPS
GMLP_CHUNK = 128

CONF_WIDTH = D_MODEL
CONF_KERNEL = 31
CONF_CHUNK = 128
CONF_HIST = 32

HGRN_HEADS = 8
HGRN_EXPAND = 128
HGRN_HEAD_V = D_MODEL // HGRN_HEADS
HGRN_K = HGRN_HEADS * HGRN_EXPAND
HGRN_V = HGRN_HEADS * HGRN_HEAD_V
HGRN_CHUNK = 64

LANES = 128
SUBLANES = 8

E_Z = 0
E_XBC = E_Z + SSD_WIDTH
E_U = E_XBC + SSD_XBC
E_V = E_U + GMLP_WIDTH
E_DT = E_V + GMLP_WIDTH
E_NIN = E_DT + LANES
EVEN_MIX = SSD_WIDTH + GMLP_WIDTH

O_A = 0
O_GATE = O_A + CONF_WIDTH
O_Q = O_GATE + CONF_WIDTH
O_F = O_Q + HGRN_K
O_I = O_F + HGRN_K
O_G = O_I + HGRN_V
O_NIN = O_G + HGRN_V
ODD_MIX = CONF_WIDTH + HGRN_V

SEQ_BLOCK = 256
FFN_ROWS = 512
PROJ_COLS = 512
VMEM_LIMIT = 56 * 1024 * 1024


def _dot(a, b):
    return jnp.dot(a, b, preferred_element_type=F32)


def _dot_nt(a, b):
    return lax.dot_general(a, b, (((1,), (1,)), ((), ())), preferred_element_type=F32)


def _dot_tn(a, b):
    return lax.dot_general(a, b, (((0,), (0,)), ((), ())), preferred_element_type=F32)


def _rmsnorm(x, g):
    return x * lax.rsqrt(jnp.mean(x * x, axis=-1, keepdims=True) + NORM_EPS) * g


def _silu(x):
    return x * jax.nn.sigmoid(x)


def _gelu(x):
    return 0.5 * x * (1.0 + lax.erf(x * (2.0 ** -0.5)))


def _split_bf16(x, parts):
    out = []
    rem = x
    for _ in range(parts):
        p = rem.astype(BF16)
        out.append(p)
        rem = rem - p.astype(F32)
    return out


def _in_memory(*refs):
    z = jnp.minimum(pl.program_id(0), 0)
    return [r.at[z] for r in refs]


def _project(h, w_ref, proj_ref, row0, rows, n_cols):
    for c0 in range(0, n_cols, PROJ_COLS):
        c1 = min(c0 + PROJ_COLS, n_cols)
        proj_ref[row0:row0 + rows, c0:c1] = _dot(h, w_ref[:, c0:c1])


def _ssd_chunk(r0, m0, proj_ref, xbc_ref, y_ref, state_ref, mix_ref, convw_ref, convb_ref, dtb_ref,
               a_ref, dskip_ref, sng_ref, expand_ref, tril_ref):
    q = SSD_CHUNK
    for c0 in range(0, SSD_XBC, 2 * LANES):
        cols = slice(c0, c0 + 2 * LANES)
        pcols = slice(E_XBC + c0, E_XBC + c0 + 2 * LANES)
        acc = jnp.broadcast_to(convb_ref[:, cols], (q, 2 * LANES))
        for k in range(SSD_CONV):
            acc = acc + proj_ref[pl.ds(r0 - (SSD_CONV - 1) + k, q), pcols] * convw_ref[k:k + 1, cols]
        xbc_ref[:, cols] = _silu(acc)

    dt = jax.nn.softplus(proj_ref[r0:r0 + q, E_DT:E_DT + LANES] + dtb_ref[...])
    da = dt * a_ref[...]
    cs = _dot(tril_ref[...], jnp.concatenate(_split_bf16(da, 3), axis=1))
    acs = cs[:, :LANES] + cs[:, LANES:2 * LANES] + cs[:, 2 * LANES:]
    acs_t = acs.T

    def expand(v):
        return _dot(jnp.concatenate(_split_bf16(v, 2), axis=1), expand_ref[...])

    dt_x = expand(dt)
    acs_x = expand(acs)

    row = lax.broadcasted_iota(jnp.int32, (q, q), 0)
    col = lax.broadcasted_iota(jnp.int32, (q, q), 1)
    causal = row >= col
    lane = lax.broadcasted_iota(jnp.int32, (q, LANES), 1)
    first_half = lane < SSD_HEAD_DIM

    for g in range(SSD_GROUPS):
        gc = slice(g * GROUP_W, (g + 1) * GROUP_W)
        acs_g = acs_x[:, gc]
        last_g = acs_g[q - 1:q, :]
        xs_g = xbc_ref[:, gc]
        xdt_g = xs_g * dt_x[:, gc]
        xdt_b = xdt_g.astype(BF16)
        xw_b = (xdt_g * jnp.exp(last_g - acs_g)).astype(BF16)
        bm_b = xbc_ref[:, SSD_WIDTH + g * SSD_STATE:SSD_WIDTH + (g + 1) * SSD_STATE].astype(BF16)
        cm0 = SSD_WIDTH + SSD_GROUPS * SSD_STATE + g * SSD_STATE
        cm_b = xbc_ref[:, cm0:cm0 + SSD_STATE].astype(BF16)
        st = state_ref[g]
        y_off = _dot(cm_b, st.astype(BF16)) * jnp.exp(acs_g)
        state_ref[g] = st * jnp.exp(last_g) + _dot_tn(bm_b, xw_b)
        cb = _dot_nt(cm_b, bm_b)
        for jp in range(SSD_HPG // 2):
            pc = slice(jp * LANES, (jp + 1) * LANES)
            x_pair = xdt_b[:, pc]
            yd = []
            for jj in range(2):
                hd = g * SSD_HPG + 2 * jp + jj
                seg = acs[:, hd:hd + 1] - acs_t[hd:hd + 1, :]
                m = (cb * jnp.where(causal, jnp.exp(seg), 0.0)).astype(BF16)
                yd.append(_dot(m, x_pair))
            y_diag = jnp.where(first_half, yd[0], yd[1])
            yc = slice(g * GROUP_W + jp * LANES, g * GROUP_W + (jp + 1) * LANES)
            y_ref[:, yc] = y_diag + y_off[:, pc] + dskip_ref[:, yc] * xs_g[:, pc]

    for g in range(SSD_GROUPS):
        gc = slice(g * GROUP_W, (g + 1) * GROUP_W)
        yg = y_ref[:, gc] * _silu(proj_ref[r0:r0 + q, E_Z + g * GROUP_W:E_Z + (g + 1) * GROUP_W])
        yg = yg * lax.rsqrt(jnp.mean(yg * yg, axis=-1, keepdims=True) + NORM_EPS)
        mix_ref[m0:m0 + q, gc] = (yg * sng_ref[:, gc]).astype(BF16)


def _gmlp_chunk(r0, m0, proj_ref, u_ref, v_ref, mix_ref, lng_ref, lnb_ref, wsp_ref, bsp_ref):
    q = GMLP_CHUNK
    tot = jnp.zeros((q, 1), F32)
    for c0 in range(0, GMLP_WIDTH, 2 * LANES):
        cols = slice(c0, c0 + 2 * LANES)
        u_ref[:, cols] = _gelu(proj_ref[r0:r0 + q, E_U + c0:E_U + c0 + 2 * LANES])
        v = _gelu(proj_ref[r0:r0 + q, E_V + c0:E_V + c0 + 2 * LANES])
        v_ref[:, cols] = v
        tot = tot + jnp.sum(v, axis=-1, keepdims=True)
    mu = tot * (1.0 / GMLP_WIDTH)
    sq = jnp.zeros((q, 1), F32)
    for c0 in range(0, GMLP_WIDTH, 2 * LANES):
        dv = v_ref[:, c0:c0 + 2 * LANES] - mu
        sq = sq + jnp.sum(dv * dv, axis=-1, keepdims=True)
    inv = lax.rsqrt(sq * (1.0 / GMLP_WIDTH) + NORM_EPS)
    for g in range(GMLP_GROUPS):
        cols = slice(g * GMLP_GROUP_DIM, (g + 1) * GMLP_GROUP_DIM)
        vn = ((v_ref[:, cols] - mu) * inv * lng_ref[:, cols] + lnb_ref[:, cols]).astype(BF16)
        mixed = _dot(wsp_ref[g], vn) + bsp_ref[:, cols]
        mix_ref[m0:m0 + q, SSD_WIDTH + g * GMLP_GROUP_DIM:SSD_WIDTH + (g + 1) * GMLP_GROUP_DIM] = (
            u_ref[:, cols] * mixed).astype(BF16)


def _even_kernel(x_ref, ng_ref, win_ref, wout_ref, convw_ref, convb_ref, dtb_ref, a_ref, dskip_ref,
                 sng_ref, lng_ref, lnb_ref, wsp_ref, bsp_ref, expand_ref, tril_ref, o_ref,
                 proj_ref, mix_ref, xbc_ref, y_ref, u_ref, v_ref, state_ref):
    ts = x_ref.shape[1]
    pad = SUBLANES
    xbc_cols = slice(E_XBC, E_XBC + SSD_XBC)

    @pl.when(pl.program_id(1) == 0)
    def _():
        state_ref[...] = jnp.zeros(state_ref.shape, F32)
        proj_ref[0:pad, xbc_cols] = jnp.zeros((pad, SSD_XBC), F32)

    @pl.when(pl.program_id(1) > 0)
    def _():
        proj_ref[0:pad, xbc_cols] = proj_ref[ts:ts + pad, xbc_cols]

    xb = x_ref[0]
    h = _rmsnorm(xb, ng_ref[...]).astype(BF16)
    _project(h, win_ref, proj_ref, pad, ts, E_NIN)

    for c in range(ts // SSD_CHUNK):
        _ssd_chunk(pad + c * SSD_CHUNK, c * SSD_CHUNK, proj_ref, xbc_ref, y_ref, state_ref, mix_ref,
                   convw_ref, convb_ref, dtb_ref, a_ref, dskip_ref, sng_ref, expand_ref, tril_ref)
        _gmlp_chunk(pad + c * GMLP_CHUNK, c * GMLP_CHUNK, proj_ref, u_ref, v_ref, mix_ref,
                    lng_ref, lnb_ref, wsp_ref, bsp_ref)

    o_ref[0] = xb + _dot(mix_ref[...], wout_ref[...])


def _conf_chunk(r0, m0, proj_ref, glu_ref, cv_ref, mix_ref, convw_ref, convb_ref, lng_ref, lnb_ref):
    q = CONF_CHUNK
    g0 = CONF_HIST + m0
    win = CONF_HIST + q
    lead = CONF_HIST - (CONF_KERNEL - 1)
    for c0 in range(0, CONF_WIDTH, 2 * LANES):
        a = proj_ref[r0:r0 + q, O_A + c0:O_A + c0 + 2 * LANES]
        gate = proj_ref[r0:r0 + q, O_GATE + c0:O_GATE + c0 + 2 * LANES]
        glu_ref[g0:g0 + q, c0:c0 + 2 * LANES] = a * jax.nn.sigmoid(gate)
    tot = jnp.zeros((q, 1), F32)
    for c0 in range(0, CONF_WIDTH, 2 * LANES):
        cols = slice(c0, c0 + 2 * LANES)
        window = glu_ref[g0 - CONF_HIST:g0 + q, cols]
        acc = jnp.broadcast_to(convb_ref[:, cols], (q, 2 * LANES))
        for res in range(SUBLANES):
            rolled = window if res == 0 else pltpu.roll(window, win - res, axis=0)
            for a8 in range(0, win - q + 1, SUBLANES):
                k = a8 + res - lead
                if 0 <= k < CONF_KERNEL:
                    acc = acc + rolled[a8:a8 + q, :] * convw_ref[k:k + 1, cols]
        cv_ref[:, cols] = acc
        tot = tot + jnp.sum(acc, axis=-1, keepdims=True)
    mu = tot * (1.0 / CONF_WIDTH)
    sq = jnp.zeros((q, 1), F32)
    for c0 in range(0, CONF_WIDTH, 2 * LANES):
        dv = cv_ref[:, c0:c0 + 2 * LANES] - mu
        sq = sq + jnp.sum(dv * dv, axis=-1, keepdims=True)
    inv = lax.rsqrt(sq * (1.0 / CONF_WIDTH) + NORM_EPS)
    for c0 in range(0, CONF_WIDTH, 2 * LANES):
        cols = slice(c0, c0 + 2 * LANES)
        hn = (cv_ref[:, cols] - mu) * inv * lng_ref[:, cols] + lnb_ref[:, cols]
        mix_ref[m0:m0 + q, cols] = _silu(hn).astype(BF16)


def _hgrn_block(ts, proj_ref, mix_ref, state_ref, lb_ref, hng_ref, tril_ref, lf_ref, kk_ref, bc_ref,
                qg_ref, kg_ref, qd_ref, kd_ref, cd_ref, iv_ref):
    c = HGRN_CHUNK
    lf_ref, kk_ref, bc_ref, qg_ref, kg_ref, qd_ref, kd_ref, cd_ref, iv_ref = _in_memory(
        lf_ref, kk_ref, bc_ref, qg_ref, kg_ref, qd_ref, kd_ref, cd_ref, iv_ref)
    heads = [(slice(hd * HGRN_EXPAND, (hd + 1) * HGRN_EXPAND),
              slice(hd * HGRN_HEAD_V, (hd + 1) * HGRN_HEAD_V)) for hd in range(HGRN_HEADS)]

    for kc, _ in heads:
        lb = lb_ref[:, kc]
        one_m_lb = 1.0 - lb
        fx = proj_ref[0:ts, O_F + kc.start:O_F + kc.stop]
        e = jnp.exp(-jnp.abs(fx))
        rcp = 1.0 / (1.0 + e)
        e_rcp = e * rcp
        pos = fx >= 0.0
        log_f = jnp.log(lb + one_m_lb * jnp.where(pos, rcp, e_rcp))
        kk_ref[:, kc] = one_m_lb * jnp.where(pos, e_rcp, rcp)
        hi, lo = _split_bf16(log_f, 2)
        lf_ref[:, kc] = hi
        lf_ref[:, HGRN_K + kc.start:HGRN_K + kc.stop] = lo

    for c0 in range(0, HGRN_K, PROJ_COLS):
        bc_ref[:, c0:c0 + PROJ_COLS] = (_dot(tril_ref[...], lf_ref[:, c0:c0 + PROJ_COLS])
                                        + _dot(tril_ref[...], lf_ref[:, HGRN_K + c0:HGRN_K + c0 + PROJ_COLS]))

    for ch in range(ts // c):
        rows = slice(ch * c, (ch + 1) * c)
        for kc, _ in heads:
            bcum = bc_ref[rows, kc]
            mid = bc_ref[ch * c + c // 2:ch * c + c // 2 + 1, kc]
            last = bc_ref[ch * c + c - 1:ch * c + c, kc]
            qv = proj_ref[rows, O_Q + kc.start:O_Q + kc.stop] * (HGRN_EXPAND ** -0.5)
            qg = qv * jnp.exp(bcum - mid)
            kg = kk_ref[rows, kc] * jnp.exp(mid - bcum)
            qg_ref[rows, kc] = qg.astype(BF16)
            kg_ref[rows, kc] = kg.astype(BF16)
            qd_ref[rows, kc] = (qg * jnp.exp(mid)).astype(BF16)
            kd_ref[rows, kc] = (kg * jnp.exp(last - mid)).astype(BF16)
            cd_ref[ch, :, kc] = jnp.exp(last)
    for _, vc in heads:
        iv_ref[:, vc] = proj_ref[0:ts, O_I + vc.start:O_I + vc.stop].astype(BF16)

    row = lax.broadcasted_iota(jnp.int32, (c, c), 0)
    col = lax.broadcasted_iota(jnp.int32, (c, c), 1)
    causal = row >= col

    for ch in range(ts // c):
        rows = slice(ch * c, (ch + 1) * c)
        for hd, (kc, vc) in enumerate(heads):
            att = jnp.where(causal, _dot_nt(qg_ref[rows, kc], kg_ref[rows, kc]), 0.0).astype(BF16)
            iv_b = iv_ref[rows, vc]
            st = state_ref[hd]
            o = _dot(att, iv_b) + _dot_nt(qd_ref[rows, kc], st.astype(BF16))
            state_ref[hd] = st * cd_ref[ch, :, kc] + _dot_tn(iv_b, kd_ref[rows, kc])
            o = o * lax.rsqrt(jnp.mean(o * o, axis=-1, keepdims=True) + NORM_EPS) * hng_ref[:, vc]
            gx = proj_ref[rows, O_G + vc.start:O_G + vc.stop]
            mix_ref[rows, CONF_WIDTH + vc.start:CONF_WIDTH + vc.stop] = (o * _silu(gx)).astype(BF16)


def _odd_kernel(x_ref, ng_ref, win_ref, wout_ref, convw_ref, convb_ref, lng_ref, lnb_ref, lb_ref,
                hng_ref, tril_ref, o_ref, proj_ref, mix_ref, glu_ref, cv_ref, state_ref,
                lf_ref, kk_ref, bc_ref, qg_ref, kg_ref, qd_ref, kd_ref, cd_ref, iv_ref):
    ts = x_ref.shape[1]

    @pl.when(pl.program_id(1) == 0)
    def _():
        state_ref[...] = jnp.zeros(state_ref.shape, F32)
        glu_ref[0:CONF_HIST, :] = jnp.zeros((CONF_HIST, CONF_WIDTH), F32)

    @pl.when(pl.program_id(1) > 0)
    def _():
        glu_ref[0:CONF_HIST, :] = glu_ref[ts:ts + CONF_HIST, :]

    xb = x_ref[0]
    h = _rmsnorm(xb, ng_ref[...]).astype(BF16)
    _project(h, win_ref, proj_ref, 0, ts, O_NIN)

    for c in range(ts // CONF_CHUNK):
        _conf_chunk(c * CONF_CHUNK, c * CONF_CHUNK, proj_ref, glu_ref, cv_ref, mix_ref,
                    convw_ref, convb_ref, lng_ref, lnb_ref)
    _hgrn_block(ts, proj_ref, mix_ref, state_ref, lb_ref, hng_ref, tril_ref, lf_ref, kk_ref, bc_ref,
                qg_ref, kg_ref, qd_ref, kd_ref, cd_ref, iv_ref)

    o_ref[0] = xb + _dot(mix_ref[...], wout_ref[...])


def _ffn_kernel(x_ref, ng_ref, w1_ref, w2_ref, fg_ref, o_ref, hid_ref, *, final_norm):
    xb = x_ref[...]
    h = _rmsnorm(xb, ng_ref[...]).astype(BF16)
    for c0 in range(0, D_FF, PROJ_COLS):
        a = jnp.maximum(_dot(h, w1_ref[:, c0:c0 + PROJ_COLS]), 0.0)
        hid_ref[:, c0:c0 + PROJ_COLS] = (a * a).astype(BF16)
    y = xb + _dot(hid_ref[...], w2_ref[...])
    if final_norm:
        y = _rmsnorm(y, fg_ref[...])
    o_ref[...] = y


def _const_spec(shape):
    zeros = (0,) * len(shape)
    return pl.BlockSpec(shape, lambda *_: zeros, pipeline_mode=pl.Buffered(1))


def _mixer_call(body, x, consts, scratch, name):
    bsz, seq, d = x.shape
    ts = SEQ_BLOCK
    x_spec = pl.BlockSpec((1, ts, d), lambda b, s: (b, s, 0))
    return pl.pallas_call(
        body,
        grid=(bsz, seq // ts),
        in_specs=[x_spec] + [_const_spec(c.shape) for c in consts],
        out_specs=x_spec,
        out_shape=jax.ShapeDtypeStruct(x.shape, x.dtype),
        scratch_shapes=scratch,
        compiler_params=pltpu.CompilerParams(dimension_semantics=("arbitrary", "arbitrary"),
                                             vmem_limit_bytes=VMEM_LIMIT),
        name=name,
    )(x, *consts)


def _row(v):
    return v.reshape(1, -1).astype(F32)


def _lane_pad(v):
    return jnp.pad(_row(v), ((0, 0), (0, LANES - v.shape[-1])))


def _even_layer(x, ng, w_in, w_out, conv_w, conv_b, dt_bias, a_log, d_skip, ssd_ng, ln_g, ln_b, w_s, b_s):
    dt0 = SSD_WIDTH + SSD_XBC
    w_in_r = jnp.concatenate(
        [w_in[:, :dt0], w_in[:, dt0 + SSD_HEADS:], w_in[:, dt0:dt0 + SSD_HEADS],
         jnp.zeros((D_MODEL, LANES - SSD_HEADS), w_in.dtype)], axis=1).astype(BF16)
    tri = np.tril(np.ones((GMLP_CHUNK, GMLP_CHUNK), np.float32))
    w_sp = jnp.where(tri[None] > 0, w_s, 0.0).astype(BF16)
    b_sp = jnp.repeat(jnp.transpose(b_s).astype(F32), GMLP_GROUP_DIM, axis=1)
    expand = np.zeros((2 * LANES, SSD_WIDTH), np.float32)
    for part in range(2):
        for hd in range(SSD_HEADS):
            expand[part * LANES + hd, hd * SSD_HEAD_DIM:(hd + 1) * SSD_HEAD_DIM] = 1.0
    consts = [
        _row(ng), w_in_r, w_out.astype(BF16), conv_w.astype(F32), _row(conv_b), _lane_pad(dt_bias),
        _lane_pad(-jnp.exp(a_log.astype(F32))), _row(jnp.repeat(d_skip.astype(F32), SSD_HEAD_DIM)),
        _row(ssd_ng), _row(ln_g), _row(ln_b), w_sp, b_sp,
        jnp.asarray(expand, BF16), jnp.asarray(tri, BF16),
    ]
    ts = SEQ_BLOCK
    scratch = [
        pltpu.VMEM((SUBLANES + ts, E_NIN), F32),
        pltpu.VMEM((ts, EVEN_MIX), BF16),
        pltpu.VMEM((SSD_CHUNK, SSD_XBC), F32),
        pltpu.VMEM((SSD_CHUNK, SSD_WIDTH), F32),
        pltpu.VMEM((GMLP_CHUNK, GMLP_WIDTH), F32),
        pltpu.VMEM((GMLP_CHUNK, GMLP_WIDTH), F32),
        pltpu.VMEM((SSD_GROUPS, SSD_STATE, GROUP_W), F32),
    ]
    return _mixer_call(_even_kernel, x, consts, scratch, "even_mixer")


def _odd_layer(x, ng, w_in, w_out, conv_w, conv_b, ln_g, ln_b, lb, hgrn_ng):
    ts = SEQ_BLOCK
    tri = np.kron(np.eye(ts // HGRN_CHUNK, dtype=np.float32),
                  np.tril(np.ones((HGRN_CHUNK, HGRN_CHUNK), np.float32)))
    consts = [
        _row(ng), w_in.astype(BF16), w_out.astype(BF16), conv_w.astype(F32), _row(conv_b),
        _row(ln_g), _row(ln_b), _row(lb), _row(hgrn_ng), jnp.asarray(tri, BF16),
    ]
    scratch = [
        pltpu.VMEM((ts, O_NIN), F32),
        pltpu.VMEM((ts, ODD_MIX), BF16),
        pltpu.VMEM((CONF_HIST + ts, CONF_WIDTH), F32),
        pltpu.VMEM((CONF_CHUNK, CONF_WIDTH), F32),
        pltpu.VMEM((HGRN_HEADS, HGRN_HEAD_V, HGRN_EXPAND), F32),
        pltpu.VMEM((1, ts, 2 * HGRN_K), BF16),
        pltpu.VMEM((1, ts, HGRN_K), F32),
        pltpu.VMEM((1, ts, HGRN_K), F32),
        pltpu.VMEM((1, ts, HGRN_K), BF16),
        pltpu.VMEM((1, ts, HGRN_K), BF16),
        pltpu.VMEM((1, ts, HGRN_K), BF16),
        pltpu.VMEM((1, ts, HGRN_K), BF16),
        pltpu.VMEM((1, ts // HGRN_CHUNK, 1, HGRN_K), F32),
        pltpu.VMEM((1, ts, HGRN_V), BF16),
    ]
    return _mixer_call(_odd_kernel, x, consts, scratch, "odd_mixer")


def _ffn_layer(x2d, ng, w1, w2, final_g, final_norm):
    rows, d = x2d.shape
    tm = FFN_ROWS
    x_spec = pl.BlockSpec((tm, d), lambda i: (i, 0))
    consts = [_row(ng), w1.astype(BF16), w2.astype(BF16), _row(final_g)]
    return pl.pallas_call(
        functools.partial(_ffn_kernel, final_norm=final_norm),
        grid=(rows // tm,),
        in_specs=[x_spec] + [_const_spec(c.shape) for c in consts],
        out_specs=x_spec,
        out_shape=jax.ShapeDtypeStruct(x2d.shape, x2d.dtype),
        scratch_shapes=[pltpu.VMEM((tm, D_FF), BF16)],
        compiler_params=pltpu.CompilerParams(dimension_semantics=("arbitrary",),
                                             vmem_limit_bytes=VMEM_LIMIT),
        name="ffn",
    )(x2d, *consts)


def kernel(x, even_w_in, even_w_out, ssd_conv_w, ssd_conv_b, ssd_dt_bias, ssd_a_log, ssd_d, ssd_norm_g, gmlp_ln_g, gmlp_ln_b, gmlp_w_s, gmlp_b_s, odd_w_in, odd_w_out, conf_conv_w, conf_conv_b, conf_ln_g, conf_ln_b, hgrn_lb_logits, hgrn_norm_g, mix_norm_g, ffn_norm_g, ffn_w1, ffn_w2, final_norm_g):
    bsz, seq, d = x.shape
    depth = mix_norm_g.shape[0]
    assert d == D_MODEL and seq % SEQ_BLOCK == 0 and (bsz * seq) % FFN_ROWS == 0
    lb_all = jnp.cumsum(jax.nn.softmax(hgrn_lb_logits.astype(F32), axis=0), axis=0)
    lb_all = lb_all - lb_all[:1]
    for layer in range(depth):
        i = layer // 2
        if layer % 2 == 0:
            x = _even_layer(x, mix_norm_g[layer], even_w_in[i], even_w_out[i], ssd_conv_w[i], ssd_conv_b[i],
                            ssd_dt_bias[i], ssd_a_log[i], ssd_d[i], ssd_norm_g[i], gmlp_ln_g[i],
                            gmlp_ln_b[i], gmlp_w_s[i], gmlp_b_s[i])
        else:
            x = _odd_layer(x, mix_norm_g[layer], odd_w_in[i], odd_w_out[i], conf_conv_w[i], conf_conv_b[i],
                           conf_ln_g[i], conf_ln_b[i], lb_all[i], hgrn_norm_g[i])
        x = _ffn_layer(x.reshape(bsz * seq, d), ffn_norm_g[layer], ffn_w1[layer], ffn_w2[layer],
                       final_norm_g, layer == depth - 1).reshape(bsz, seq, d)
    return x
```

```python
import functools

import jax
import jax.numpy as jnp
import numpy as np
from jax import lax
from jax.experimental import pallas as pl
from jax.experimental.pallas import tpu as pltpu

F32 = jnp.float32
BF16 = jnp.bfloat16

D_MODEL = 1024
D_FF = 4 * D_MODEL
NORM_EPS = 1e-5

SSD_WIDTH = D_MODEL
SSD_HEAD_DIM = 64
SSD_HEADS = SSD_WIDTH // SSD_HEAD_DIM
SSD_GROUPS = 2
SSD_HPG = SSD_HEADS // SSD_GROUPS
SSD_STATE = 128
SSD_CONV = 4
SSD_CHUNK = 128
SSD_XBC = SSD_WIDTH + 2 * SSD_GROUPS * SSD_STATE
GROUP_W = SSD_HPG * SSD_HEAD_DIM

GMLP_WIDTH = D_MODEL
GMLP_GROUPS = 8
GMLP_GROUP_DIM = GMLP_WIDTH // GMLP_GROUPS
GMLP_CHUNK = 128

CONF_WIDTH = D_MODEL
CONF_KERNEL = 31
CONF_CHUNK = 128
CONF_HIST = 32

HGRN_HEADS = 8
HGRN_EXPAND = 128
HGRN_HEAD_V = D_MODEL // HGRN_HEADS
HGRN_K = HGRN_HEADS * HGRN_EXPAND
HGRN_V = HGRN_HEADS * HGRN_HEAD_V
HGRN_CHUNK = 64

LANES = 128
SUBLANES = 8

E_Z = 0
E_XBC = E_Z + SSD_WIDTH
E_U = E_XBC + SSD_XBC
E_V = E_U + GMLP_WIDTH
E_DT = E_V + GMLP_WIDTH
E_NIN = E_DT + LANES
EVEN_MIX = SSD_WIDTH + GMLP_WIDTH

CONF_TILE = 2 * LANES
O_GLU = 0
O_Q = O_GLU + 2 * CONF_WIDTH
O_F = O_Q + HGRN_K
O_I = O_F + HGRN_K
O_G = O_I + HGRN_V
O_NIN = O_G + HGRN_V
ODD_MIX = CONF_WIDTH + HGRN_V

SEQ_BLOCK = 512
FFN_ROWS = 1024
PROJ_COLS = 512
VMEM_LIMIT = 56 * 1024 * 1024


def _dot(a, b):
    return jnp.dot(a, b, preferred_element_type=F32)


def _dot_nt(a, b):
    return lax.dot_general(a, b, (((1,), (1,)), ((), ())), preferred_element_type=F32)


def _dot_tn(a, b):
    return lax.dot_general(a, b, (((0,), (0,)), ((), ())), preferred_element_type=F32)


def _rmsnorm(x, g):
    return x * lax.rsqrt(jnp.mean(x * x, axis=-1, keepdims=True) + NORM_EPS) * g


def _silu(x):
    return x * jax.nn.sigmoid(x)


def _gelu(x):
    return 0.5 * x * (1.0 + lax.erf(x * (2.0 ** -0.5)))


def _split_bf16(x, parts):
    out = []
    rem = x
    for _ in range(parts):
        p = rem.astype(BF16)
        out.append(p)
        rem = rem - p.astype(F32)
    return out


def _in_memory(*refs):
    z = jnp.minimum(pl.program_id(0), 0)
    return [r.at[z] for r in refs]


def _project(h, w_ref, proj_ref, row0, rows, n_cols):
    for c0 in range(0, n_cols, PROJ_COLS):
        c1 = min(c0 + PROJ_COLS, n_cols)
        proj_ref[row0:row0 + rows, c0:c1] = _dot(h, w_ref[:, c0:c1])


def _ssd_chunk(r0, m0, proj_ref, xbc_ref, y_ref, state_ref, mix_ref, convw_ref, convb_ref, dtb_ref,
               a_ref, dskip_ref, sng_ref, expand_ref, tril_ref):
    q = SSD_CHUNK
    for c0 in range(0, SSD_XBC, 2 * LANES):
        cols = slice(c0, c0 + 2 * LANES)
        pcols = slice(E_XBC + c0, E_XBC + c0 + 2 * LANES)
        acc = jnp.broadcast_to(convb_ref[:, cols], (q, 2 * LANES))
        for k in range(SSD_CONV):
            acc = acc + proj_ref[pl.ds(r0 - (SSD_CONV - 1) + k, q), pcols] * convw_ref[k:k + 1, cols]
        xbc_ref[:, cols] = _silu(acc)

    dt = jax.nn.softplus(proj_ref[r0:r0 + q, E_DT:E_DT + LANES] + dtb_ref[...])
    da = dt * a_ref[...]
    cs = _dot(tril_ref[...], jnp.concatenate(_split_bf16(da, 3), axis=1))
    acs = cs[:, :LANES] + cs[:, LANES:2 * LANES] + cs[:, 2 * LANES:]
    acs_t = acs.T

    def expand(v):
        return _dot(jnp.concatenate(_split_bf16(v, 2), axis=1), expand_ref[...])

    dt_x = expand(dt)
    acs_x = expand(acs)

    row = lax.broadcasted_iota(jnp.int32, (q, q), 0)
    col = lax.broadcasted_iota(jnp.int32, (q, q), 1)
    causal = row >= col
    lane = lax.broadcasted_iota(jnp.int32, (q, LANES), 1)
    first_half = lane < SSD_HEAD_DIM

    for g in range(SSD_GROUPS):
        gc = slice(g * GROUP_W, (g + 1) * GROUP_W)
        acs_g = acs_x[:, gc]
        last_g = acs_g[q - 1:q, :]
        xs_g = xbc_ref[:, gc]
        xdt_g = xs_g * dt_x[:, gc]
        xdt_b = xdt_g.astype(BF16)
        xw_b = (xdt_g * jnp.exp(last_g - acs_g)).astype(BF16)
        bm_b = xbc_ref[:, SSD_WIDTH + g * SSD_STATE:SSD_WIDTH + (g + 1) * SSD_STATE].astype(BF16)
        cm0 = SSD_WIDTH + SSD_GROUPS * SSD_STATE + g * SSD_STATE
        cm_b = xbc_ref[:, cm0:cm0 + SSD_STATE].astype(BF16)
        st = state_ref[g]
        y_off = _dot(cm_b, st.astype(BF16)) * jnp.exp(acs_g)
        state_ref[g] = st * jnp.exp(last_g) + _dot_tn(bm_b, xw_b)
        cb = _dot_nt(cm_b, bm_b)
        for jp in range(SSD_HPG // 2):
            pc = slice(jp * LANES, (jp + 1) * LANES)
            x_pair = xdt_b[:, pc]
            yd = []
            for jj in range(2):
                hd = g * SSD_HPG + 2 * jp + jj
                seg = acs[:, hd:hd + 1] - acs_t[hd:hd + 1, :]
                m = (cb * jnp.where(causal, jnp.exp(seg), 0.0)).astype(BF16)
                yd.append(_dot(m, x_pair))
            y_diag = jnp.where(first_half, yd[0], yd[1])
            yc = slice(g * GROUP_W + jp * LANES, g * GROUP_W + (jp + 1) * LANES)
            y_ref[:, yc] = y_diag + y_off[:, pc] + dskip_ref[:, yc] * xs_g[:, pc]

    for g in range(SSD_GROUPS):
        gc = slice(g * GROUP_W, (g + 1) * GROUP_W)
        yg = y_ref[:, gc] * _silu(proj_ref[r0:r0 + q, E_Z + g * GROUP_W:E_Z + (g + 1) * GROUP_W])
        yg = yg * lax.rsqrt(jnp.mean(yg * yg, axis=-1, keepdims=True) + NORM_EPS)
        mix_ref[m0:m0 + q, gc] = (yg * sng_ref[:, gc]).astype(BF16)


def _gmlp_chunk(r0, m0, proj_ref, u_ref, v_ref, mix_ref, lng_ref, lnb_ref, wsp_ref, bsp_ref):
    q = GMLP_CHUNK
    tot = jnp.zeros((q, 1), F32)
    for c0 in range(0, GMLP_WIDTH, 2 * LANES):
        cols = slice(c0, c0 + 2 * LANES)
        u_ref[:, cols] = _gelu(proj_ref[r0:r0 + q, E_U + c0:E_U + c0 + 2 * LANES])
        v = _gelu(proj_ref[r0:r0 + q, E_V + c0:E_V + c0 + 2 * LANES])
        v_ref[:, cols] = v
        tot = tot + jnp.sum(v, axis=-1, keepdims=True)
    mu = tot * (1.0 / GMLP_WIDTH)
    sq = jnp.zeros((q, 1), F32)
    for c0 in range(0, GMLP_WIDTH, 2 * LANES):
        dv = v_ref[:, c0:c0 + 2 * LANES] - mu
        sq = sq + jnp.sum(dv * dv, axis=-1, keepdims=True)
    inv = lax.rsqrt(sq * (1.0 / GMLP_WIDTH) + NORM_EPS)
    for g in range(GMLP_GROUPS):
        cols = slice(g * GMLP_GROUP_DIM, (g + 1) * GMLP_GROUP_DIM)
        vn = ((v_ref[:, cols] - mu) * inv * lng_ref[:, cols] + lnb_ref[:, cols]).astype(BF16)
        mixed = _dot(wsp_ref[g], vn) + bsp_ref[:, cols]
        mix_ref[m0:m0 + q, SSD_WIDTH + g * GMLP_GROUP_DIM:SSD_WIDTH + (g + 1) * GMLP_GROUP_DIM] = (
            u_ref[:, cols] * mixed).astype(BF16)


def _even_kernel(x_ref, ng_ref, win_ref, wout_ref, convw_ref, convb_ref, dtb_ref, a_ref, dskip_ref,
                 sng_ref, lng_ref, lnb_ref, wsp_ref, bsp_ref, expand_ref, tril_ref, o_ref,
                 proj_ref, mix_ref, xbc_ref, y_ref, u_ref, v_ref, state_ref):
    ts = x_ref.shape[1]
    pad = SUBLANES
    xbc_cols = slice(E_XBC, E_XBC + SSD_XBC)

    @pl.when(pl.program_id(1) == 0)
    def _():
        state_ref[...] = jnp.zeros(state_ref.shape, F32)
        proj_ref[0:pad, xbc_cols] = jnp.zeros((pad, SSD_XBC), F32)

    @pl.when(pl.program_id(1) > 0)
    def _():
        proj_ref[0:pad, xbc_cols] = proj_ref[ts:ts + pad, xbc_cols]

    xb = x_ref[0]
    h = _rmsnorm(xb, ng_ref[...]).astype(BF16)
    _project(h, win_ref, proj_ref, pad, ts, E_NIN)

    for c in range(ts // SSD_CHUNK):
        _ssd_chunk(pad + c * SSD_CHUNK, c * SSD_CHUNK, proj_ref, xbc_ref, y_ref, state_ref, mix_ref,
                   convw_ref, convb_ref, dtb_ref, a_ref, dskip_ref, sng_ref, expand_ref, tril_ref)
        _gmlp_chunk(pad + c * GMLP_CHUNK, c * GMLP_CHUNK, proj_ref, u_ref, v_ref, mix_ref,
                    lng_ref, lnb_ref, wsp_ref, bsp_ref)

    o_ref[0] = xb + _dot(mix_ref[...], wout_ref[...])


def _conf_chunk(r0, m0, proj_ref, glu_ref, cv_ref, mix_ref, convw_ref, convb_ref, lng_ref, lnb_ref):
    q = CONF_CHUNK
    g0 = CONF_HIST + m0
    win = CONF_HIST + q
    lead = CONF_HIST - (CONF_KERNEL - 1)
    tot = jnp.zeros((q, 1), F32)
    for c0 in range(0, CONF_WIDTH, CONF_TILE):
        cols = slice(c0, c0 + CONF_TILE)
        a = proj_ref[r0:r0 + q, O_GLU + 2 * c0:O_GLU + 2 * c0 + CONF_TILE]
        gate = proj_ref[r0:r0 + q, O_GLU + 2 * c0 + CONF_TILE:O_GLU + 2 * c0 + 2 * CONF_TILE]
        glu = a * jax.nn.sigmoid(gate)
        glu_ref[g0:g0 + q, cols] = glu
        window = jnp.concatenate([glu_ref[g0 - CONF_HIST:g0, cols], glu], axis=0)
        acc = jnp.broadcast_to(convb_ref[:, cols], (q, 2 * LANES))
        for res in range(SUBLANES):
            rolled = window if res == 0 else pltpu.roll(window, win - res, axis=0)
            for a8 in range(0, win - q + 1, SUBLANES):
                k = a8 + res - lead
                if 0 <= k < CONF_KERNEL:
                    acc = acc + rolled[a8:a8 + q, :] * convw_ref[k:k + 1, cols]
        cv_ref[:, cols] = acc
        tot = tot + jnp.sum(acc, axis=-1, keepdims=True)
    mu = tot * (1.0 / CONF_WIDTH)
    sq = jnp.zeros((q, 1), F32)
    for c0 in range(0, CONF_WIDTH, 2 * LANES):
        dv = cv_ref[:, c0:c0 + 2 * LANES] - mu
        sq = sq + jnp.sum(dv * dv, axis=-1, keepdims=True)
    inv = lax.rsqrt(sq * (1.0 / CONF_WIDTH) + NORM_EPS)
    for c0 in range(0, CONF_WIDTH, 2 * LANES):
        cols = slice(c0, c0 + 2 * LANES)
        hn = (cv_ref[:, cols] - mu) * inv * lng_ref[:, cols] + lnb_ref[:, cols]
        mix_ref[m0:m0 + q, cols] = _silu(hn).astype(BF16)


def _hgrn_block(ts, proj_ref, mix_ref, state_ref, lb_ref, hng_ref, tril_ref, lf_ref, kk_ref, bc_ref,
                qg_ref, kg_ref, qd_ref, kd_ref, cd_ref, iv_ref):
    c = HGRN_CHUNK
    lf_ref, kk_ref, bc_ref, qg_ref, kg_ref, qd_ref, kd_ref, cd_ref, iv_ref = _in_memory(
        lf_ref, kk_ref, bc_ref, qg_ref, kg_ref, qd_ref, kd_ref, cd_ref, iv_ref)
    heads = [(slice(hd * HGRN_EXPAND, (hd + 1) * HGRN_EXPAND),
              slice(hd * HGRN_HEAD_V, (hd + 1) * HGRN_HEAD_V)) for hd in range(HGRN_HEADS)]

    for kc, _ in heads:
        lb = lb_ref[:, kc]
        one_m_lb = 1.0 - lb
        fx = proj_ref[0:ts, O_F + kc.start:O_F + kc.stop]
        e = jnp.exp(-jnp.abs(fx))
        rcp = 1.0 / (1.0 + e)
        e_rcp = e * rcp
        pos = fx >= 0.0
        log_f = jnp.log(lb + one_m_lb * jnp.where(pos, rcp, e_rcp))
        kk_ref[:, kc] = one_m_lb * jnp.where(pos, e_rcp, rcp)
        hi, lo = _split_bf16(log_f, 2)
        lf_ref[:, kc] = hi
        lf_ref[:, HGRN_K + kc.start:HGRN_K + kc.stop] = lo

    for c0 in range(0, HGRN_K, PROJ_COLS):
        bc_ref[:, c0:c0 + PROJ_COLS] = (_dot(tril_ref[...], lf_ref[:, c0:c0 + PROJ_COLS])
                                        + _dot(tril_ref[...], lf_ref[:, HGRN_K + c0:HGRN_K + c0 + PROJ_COLS]))

    for ch in range(ts // c):
        rows = slice(ch * c, (ch + 1) * c)
        for kc, _ in heads:
            bcum = bc_ref[rows, kc]
            mid = bc_ref[ch * c + c // 2:ch * c + c // 2 + 1, kc]
            last = bc_ref[ch * c + c - 1:ch * c + c, kc]
            qv = proj_ref[rows, O_Q + kc.start:O_Q + kc.stop] * (HGRN_EXPAND ** -0.5)
            qg = qv * jnp.exp(bcum - mid)
            kg = kk_ref[rows, kc] * jnp.exp(mid - bcum)
            qg_ref[rows, kc] = qg.astype(BF16)
            kg_ref[rows, kc] = kg.astype(BF16)
            qd_ref[rows, kc] = (qg * jnp.exp(mid)).astype(BF16)
            kd_ref[rows, kc] = (kg * jnp.exp(last - mid)).astype(BF16)
            cd_ref[ch, :, kc] = jnp.exp(last)
    for _, vc in heads:
        iv_ref[:, vc] = proj_ref[0:ts, O_I + vc.start:O_I + vc.stop].astype(BF16)

    row = lax.broadcasted_iota(jnp.int32, (c, c), 0)
    col = lax.broadcasted_iota(jnp.int32, (c, c), 1)
    causal = row >= col

    for ch in range(ts // c):
        rows = slice(ch * c, (ch + 1) * c)
        for hd, (kc, vc) in enumerate(heads):
            att = jnp.where(causal, _dot_nt(qg_ref[rows, kc], kg_ref[rows, kc]), 0.0).astype(BF16)
            iv_b = iv_ref[rows, vc]
            st = state_ref[hd]
            o = _dot(att, iv_b) + _dot_nt(qd_ref[rows, kc], st.astype(BF16))
            state_ref[hd] = st * cd_ref[ch, :, kc] + _dot_tn(iv_b, kd_ref[rows, kc])
            o = o * lax.rsqrt(jnp.mean(o * o, axis=-1, keepdims=True) + NORM_EPS) * hng_ref[:, vc]
            gx = proj_ref[rows, O_G + vc.start:O_G + vc.stop]
            mix_ref[rows, CONF_WIDTH + vc.start:CONF_WIDTH + vc.stop] = (o * _silu(gx)).astype(BF16)


def _odd_kernel(x_ref, ng_ref, win_ref, wout_ref, convw_ref, convb_ref, lng_ref, lnb_ref, lb_ref,
                hng_ref, tril_ref, o_ref, proj_ref, mix_ref, glu_ref, cv_ref, state_ref,
                lf_ref, kk_ref, bc_ref, qg_ref, kg_ref, qd_ref, kd_ref, cd_ref, iv_ref):
    ts = x_ref.shape[1]

    @pl.when(pl.program_id(1) == 0)
    def _():
        state_ref[...] = jnp.zeros(state_ref.shape, F32)
        glu_ref[0:CONF_HIST, :] = jnp.zeros((CONF_HIST, CONF_WIDTH), F32)

    @pl.when(pl.program_id(1) > 0)
    def _():
        glu_ref[0:CONF_HIST, :] = glu_ref[ts:ts + CONF_HIST, :]

    xb = x_ref[0]
    h = _rmsnorm(xb, ng_ref[...]).astype(BF16)
    _project(h, win_ref, proj_ref, 0, ts, O_NIN)

    for c in range(ts // CONF_CHUNK):
        _conf_chunk(c * CONF_CHUNK, c * CONF_CHUNK, proj_ref, glu_ref, cv_ref, mix_ref,
                    convw_ref, convb_ref, lng_ref, lnb_ref)
    _hgrn_block(ts, proj_ref, mix_ref, state_ref, lb_ref, hng_ref, tril_ref, lf_ref, kk_ref, bc_ref,
                qg_ref, kg_ref, qd_ref, kd_ref, cd_ref, iv_ref)

    o_ref[0] = (xb + _dot(mix_ref[:, :CONF_WIDTH], wout_ref[:CONF_WIDTH, :])
                + _dot(mix_ref[:, CONF_WIDTH:], wout_ref[CONF_WIDTH:, :]))


def _ffn_kernel(x_ref, ng_ref, w1_ref, w2_ref, fg_ref, o_ref, hid_ref, *, final_norm):
    xb = x_ref[...]
    h = _rmsnorm(xb, ng_ref[...]).astype(BF16)
    for c0 in range(0, D_FF, PROJ_COLS):
        a = jnp.maximum(_dot(h, w1_ref[:, c0:c0 + PROJ_COLS]), 0.0)
        hid_ref[:, c0:c0 + PROJ_COLS] = (a * a).astype(BF16)
    y = xb + _dot(hid_ref[...], w2_ref[...])
    if final_norm:
        y = _rmsnorm(y, fg_ref[...])
    o_ref[...] = y


def _const_spec(shape):
    zeros = (0,) * len(shape)
    return pl.BlockSpec(shape, lambda *_: zeros, pipeline_mode=pl.Buffered(1))


def _mixer_call(body, x, consts, scratch, name):
    bsz, seq, d = x.shape
    ts = SEQ_BLOCK
    x_spec = pl.BlockSpec((1, ts, d), lambda b, s: (b, s, 0))
    return pl.pallas_call(
        body,
        grid=(bsz, seq // ts),
        in_specs=[x_spec] + [_const_spec(c.shape) for c in consts],
        out_specs=x_spec,
        out_shape=jax.ShapeDtypeStruct(x.shape, x.dtype),
        scratch_shapes=scratch,
        compiler_params=pltpu.CompilerParams(dimension_semantics=("arbitrary", "arbitrary"),
                                             vmem_limit_bytes=VMEM_LIMIT),
        name=name,
    )(x, *consts)


def _row(v):
    return v.reshape(1, -1).astype(F32)


def _lane_pad(v):
    return jnp.pad(_row(v), ((0, 0), (0, LANES - v.shape[-1])))


def _even_layer(x, ng, w_in, w_out, conv_w, conv_b, dt_bias, a_log, d_skip, ssd_ng, ln_g, ln_b, w_s, b_s):
    dt0 = SSD_WIDTH + SSD_XBC
    w_in_r = jnp.concatenate(
        [w_in[:, :dt0], w_in[:, dt0 + SSD_HEADS:], w_in[:, dt0:dt0 + SSD_HEADS],
         jnp.zeros((D_MODEL, LANES - SSD_HEADS), w_in.dtype)], axis=1).astype(BF16)
    tri = np.tril(np.ones((GMLP_CHUNK, GMLP_CHUNK), np.float32))
    w_sp = jnp.where(tri[None] > 0, w_s, 0.0).astype(BF16)
    b_sp = jnp.repeat(jnp.transpose(b_s).astype(F32), GMLP_GROUP_DIM, axis=1)
    expand = np.zeros((2 * LANES, SSD_WIDTH), np.float32)
    for part in range(2):
        for hd in range(SSD_HEADS):
            expand[part * LANES + hd, hd * SSD_HEAD_DIM:(hd + 1) * SSD_HEAD_DIM] = 1.0
    consts = [
        _row(ng), w_in_r, w_out.astype(BF16), conv_w.astype(F32), _row(conv_b), _lane_pad(dt_bias),
        _lane_pad(-jnp.exp(a_log.astype(F32))), _row(jnp.repeat(d_skip.astype(F32), SSD_HEAD_DIM)),
        _row(ssd_ng), _row(ln_g), _row(ln_b), w_sp, b_sp,
        jnp.asarray(expand, BF16), jnp.asarray(tri, BF16),
    ]
    ts = SEQ_BLOCK
    scratch = [
        pltpu.VMEM((SUBLANES + ts, E_NIN), F32),
        pltpu.VMEM((ts, EVEN_MIX), BF16),
        pltpu.VMEM((SSD_CHUNK, SSD_XBC), F32),
        pltpu.VMEM((SSD_CHUNK, SSD_WIDTH), F32),
        pltpu.VMEM((GMLP_CHUNK, GMLP_WIDTH), F32),
        pltpu.VMEM((GMLP_CHUNK, GMLP_WIDTH), F32),
        pltpu.VMEM((SSD_GROUPS, SSD_STATE, GROUP_W), F32),
    ]
    return _mixer_call(_even_kernel, x, consts, scratch, "even_mixer")


def _odd_layer(x, ng, w_in, w_out, conv_w, conv_b, ln_g, ln_b, lb, hgrn_ng):
    ts = SEQ_BLOCK
    pieces = CONF_WIDTH // CONF_TILE
    glu_w = jnp.stack([w_in[:, :CONF_WIDTH].reshape(D_MODEL, pieces, CONF_TILE),
                       w_in[:, CONF_WIDTH:2 * CONF_WIDTH].reshape(D_MODEL, pieces, CONF_TILE)], axis=2)
    w_in_r = jnp.concatenate([glu_w.reshape(D_MODEL, 2 * CONF_WIDTH), w_in[:, 2 * CONF_WIDTH:]],
                             axis=1).astype(BF16)
    tri = np.kron(np.eye(ts // HGRN_CHUNK, dtype=np.float32),
                  np.tril(np.ones((HGRN_CHUNK, HGRN_CHUNK), np.float32)))
    consts = [
        _row(ng), w_in_r, w_out.astype(BF16), conv_w.astype(F32), _row(conv_b),
        _row(ln_g), _row(ln_b), _row(lb), _row(hgrn_ng), jnp.asarray(tri, BF16),
    ]
    scratch = [
        pltpu.VMEM((ts, O_NIN), F32),
        pltpu.VMEM((ts, ODD_MIX), BF16),
        pltpu.VMEM((CONF_HIST + ts, CONF_WIDTH), F32),
        pltpu.VMEM((CONF_CHUNK, CONF_WIDTH), F32),
        pltpu.VMEM((HGRN_HEADS, HGRN_HEAD_V, HGRN_EXPAND), F32),
        pltpu.VMEM((1, ts, 2 * HGRN_K), BF16),
        pltpu.VMEM((1, ts, HGRN_K), F32),
        pltpu.VMEM((1, ts, HGRN_K), F32),
        pltpu.VMEM((1, ts, HGRN_K), BF16),
        pltpu.VMEM((1, ts, HGRN_K), BF16),
        pltpu.VMEM((1, ts, HGRN_K), BF16),
        pltpu.VMEM((1, ts, HGRN_K), BF16),
        pltpu.VMEM((1, ts // HGRN_CHUNK, 1, HGRN_K), F32),
        pltpu.VMEM((1, ts, HGRN_V), BF16),
    ]
    return _mixer_call(_odd_kernel, x, consts, scratch, "odd_mixer")


def _ffn_layer(x2d, ng, w1, w2, final_g, final_norm):
    rows, d = x2d.shape
    tm = FFN_ROWS
    x_spec = pl.BlockSpec((tm, d), lambda i: (i, 0))
    consts = [_row(ng), w1.astype(BF16), w2.astype(BF16), _row(final_g)]
    return pl.pallas_call(
        functools.partial(_ffn_kernel, final_norm=final_norm),
        grid=(rows // tm,),
        in_specs=[x_spec] + [_const_spec(c.shape) for c in consts],
        out_specs=x_spec,
        out_shape=jax.ShapeDtypeStruct(x2d.shape, x2d.dtype),
        scratch_shapes=[pltpu.VMEM((tm, D_FF), BF16)],
        compiler_params=pltpu.CompilerParams(dimension_semantics=("arbitrary",),
                                             vmem_limit_bytes=VMEM_LIMIT),
        name="ffn",
    )(x2d, *consts)


def kernel(x, even_w_in, even_w_out, ssd_conv_w, ssd_conv_b, ssd_dt_bias, ssd_a_log, ssd_d, ssd_norm_g, gmlp_ln_g, gmlp_ln_b, gmlp_w_s, gmlp_b_s, odd_w_in, odd_w_out, conf_conv_w, conf_conv_b, conf_ln_g, conf_ln_b, hgrn_lb_logits, hgrn_norm_g, mix_norm_g, ffn_norm_g, ffn_w1, ffn_w2, final_norm_g):
    bsz, seq, d = x.shape
    depth = mix_norm_g.shape[0]
    assert d == D_MODEL and seq % SEQ_BLOCK == 0 and (bsz * seq) % FFN_ROWS == 0
    lb_all = jnp.cumsum(jax.nn.softmax(hgrn_lb_logits.astype(F32), axis=0), axis=0)
    lb_all = lb_all - lb_all[:1]
    for layer in range(depth):
        i = layer // 2
        if layer % 2 == 0:
            x = _even_layer(x, mix_norm_g[layer], even_w_in[i], even_w_out[i], ssd_conv_w[i], ssd_conv_b[i],
                            ssd_dt_bias[i], ssd_a_log[i], ssd_d[i], ssd_norm_g[i], gmlp_ln_g[i],
                            gmlp_ln_b[i], gmlp_w_s[i], gmlp_b_s[i])
        else:
            x = _odd_layer(x, mix_norm_g[layer], odd_w_in[i], odd_w_out[i], conf_conv_w[i], conf_conv_b[i],
                           conf_ln_g[i], conf_ln_b[i], lb_all[i], hgrn_norm_g[i])
        x = _ffn_layer(x.reshape(bsz * seq, d), ffn_norm_g[layer], ffn_w1[layer], ffn_w2[layer],
                       final_norm_g, layer == depth - 1).reshape(bsz, seq, d)
    return x
```

```python
import functools

import jax
import jax.numpy as jnp
import numpy as np
from jax import lax
from jax.experimental import pallas as pl
from jax.experimental.pallas import tpu as pltpu

F32 = jnp.float32
BF16 = jnp.bfloat16

D_MODEL = 1024
D_FF = 4 * D_MODEL
NORM_EPS = 1e-5

SSD_WIDTH = D_MODEL
SSD_HEAD_DIM = 64
SSD_HEADS = SSD_WIDTH // SSD_HEAD_DIM
SSD_GROUPS = 2
SSD_HPG = SSD_HEADS // SSD_GROUPS
SSD_STATE = 128
SSD_CONV = 4
SSD_CHUNK = 128
SSD_XBC = SSD_WIDTH + 2 * SSD_GROUPS * SSD_STATE
GROUP_W = SSD_HPG * SSD_HEAD_DIM

GMLP_WIDTH = D_MODEL
GMLP_GROUPS = 8
GMLP_GROUP_DIM = GMLP_WIDTH // GMLP_GROUPS
GMLP_CHUNK = 128

CONF_WIDTH = D_MODEL
CONF_KERNEL = 31
CONF_CHUNK = 128
CONF_HIST = 32

HGRN_HEADS = 8
HGRN_EXPAND = 128
HGRN_HEAD_V = D_MODEL // HGRN_HEADS
HGRN_K = HGRN_HEADS * HGRN_EXPAND
HGRN_V = HGRN_HEADS * HGRN_HEAD_V
HGRN_CHUNK = 64

LANES = 128
SUBLANES = 8

E_Z = 0
E_XBC = E_Z + SSD_WIDTH
E_U = E_XBC + SSD_XBC
E_V = E_U + GMLP_WIDTH
E_DT = E_V + GMLP_WIDTH
E_NIN = E_DT + LANES
EVEN_MIX = SSD_WIDTH + GMLP_WIDTH

CONF_TILE = LANES
O_GLU = 0
O_Q = O_GLU + 2 * CONF_WIDTH
O_F = O_Q + HGRN_K
O_I = O_F + HGRN_K
O_G = O_I + HGRN_V
O_NIN = O_G + HGRN_V
ODD_MIX = CONF_WIDTH + HGRN_V

SEQ_BLOCK = 512
FFN_ROWS = 1024
PROJ_COLS = 512
VMEM_LIMIT = 56 * 1024 * 1024


def _dot(a, b):
    return jnp.dot(a, b, preferred_element_type=F32)


def _dot_nt(a, b):
    return lax.dot_general(a, b, (((1,), (1,)), ((), ())), preferred_element_type=F32)


def _dot_tn(a, b):
    return lax.dot_general(a, b, (((0,), (0,)), ((), ())), preferred_element_type=F32)


def _rmsnorm(x, g):
    return x * lax.rsqrt(jnp.mean(x * x, axis=-1, keepdims=True) + NORM_EPS) * g


def _silu(x):
    return x * jax.nn.sigmoid(x)


def _gelu(x):
    return 0.5 * x * (1.0 + lax.erf(x * (2.0 ** -0.5)))


def _split_bf16(x, parts):
    out = []
    rem = x
    for _ in range(parts):
        p = rem.astype(BF16)
        out.append(p)
        rem = rem - p.astype(F32)
    return out


def _in_memory(*refs):
    z = jnp.minimum(pl.program_id(0), 0)
    return [r.at[z] for r in refs]


def _project(h, w_ref, proj_ref, row0, rows, col0=0):
    n_cols = w_ref.shape[1]
    for c0 in range(0, n_cols, PROJ_COLS):
        c1 = min(c0 + PROJ_COLS, n_cols)
        proj_ref[row0:row0 + rows, col0 + c0:col0 + c1] = _dot(h, w_ref[:, c0:c1])


def _ssd_chunk(r0, m0, proj_ref, xbc_ref, y_ref, state_ref, mix_ref, convw_ref, convb_ref, dtb_ref,
               a_ref, dskip_ref, sng_ref, expand_ref, tril_ref):
    q = SSD_CHUNK
    for c0 in range(0, SSD_XBC, 2 * LANES):
        cols = slice(c0, c0 + 2 * LANES)
        pcols = slice(E_XBC + c0, E_XBC + c0 + 2 * LANES)
        acc = jnp.broadcast_to(convb_ref[:, cols], (q, 2 * LANES))
        for k in range(SSD_CONV):
            acc = acc + proj_ref[pl.ds(r0 - (SSD_CONV - 1) + k, q), pcols] * convw_ref[k:k + 1, cols]
        xbc_ref[:, cols] = _silu(acc)

    dt = jax.nn.softplus(proj_ref[r0:r0 + q, E_DT:E_DT + LANES] + dtb_ref[...])
    da = dt * a_ref[...]
    cs = _dot(tril_ref[...], jnp.concatenate(_split_bf16(da, 3), axis=1))
    acs = cs[:, :LANES] + cs[:, LANES:2 * LANES] + cs[:, 2 * LANES:]
    acs_t = acs.T

    def expand(v):
        return _dot(jnp.concatenate(_split_bf16(v, 2), axis=1), expand_ref[...])

    dt_x = expand(dt)
    acs_x = expand(acs)

    row = lax.broadcasted_iota(jnp.int32, (q, q), 0)
    col = lax.broadcasted_iota(jnp.int32, (q, q), 1)
    causal = row >= col
    lane = lax.broadcasted_iota(jnp.int32, (q, LANES), 1)
    first_half = lane < SSD_HEAD_DIM

    for g in range(SSD_GROUPS):
        gc = slice(g * GROUP_W, (g + 1) * GROUP_W)
        acs_g = acs_x[:, gc]
        last_g = acs_g[q - 1:q, :]
        xs_g = xbc_ref[:, gc]
        xdt_g = xs_g * dt_x[:, gc]
        xdt_b = xdt_g.astype(BF16)
        xw_b = (xdt_g * jnp.exp(last_g - acs_g)).astype(BF16)
        bm_b = xbc_ref[:, SSD_WIDTH + g * SSD_STATE:SSD_WIDTH + (g + 1) * SSD_STATE].astype(BF16)
        cm0 = SSD_WIDTH + SSD_GROUPS * SSD_STATE + g * SSD_STATE
        cm_b = xbc_ref[:, cm0:cm0 + SSD_STATE].astype(BF16)
        st = state_ref[g]
        y_off = _dot(cm_b, st.astype(BF16)) * jnp.exp(acs_g)
        state_ref[g] = st * jnp.exp(last_g) + _dot_tn(bm_b, xw_b)
        cb = _dot_nt(cm_b, bm_b)
        for jp in range(SSD_HPG // 2):
            pc = slice(jp * LANES, (jp + 1) * LANES)
            x_pair = xdt_b[:, pc]
            yd = []
            for jj in range(2):
                hd = g * SSD_HPG + 2 * jp + jj
                seg = acs[:, hd:hd + 1] - acs_t[hd:hd + 1, :]
                m = (cb * jnp.where(causal, jnp.exp(seg), 0.0)).astype(BF16)
                yd.append(_dot(m, x_pair))
            y_diag = jnp.where(first_half, yd[0], yd[1])
            yc = slice(g * GROUP_W + jp * LANES, g * GROUP_W + (jp + 1) * LANES)
            y_ref[:, yc] = y_diag + y_off[:, pc] + dskip_ref[:, yc] * xs_g[:, pc]

    for g in range(SSD_GROUPS):
        gc = slice(g * GROUP_W, (g + 1) * GROUP_W)
        yg = y_ref[:, gc] * _silu(proj_ref[r0:r0 + q, E_Z + g * GROUP_W:E_Z + (g + 1) * GROUP_W])
        yg = yg * lax.rsqrt(jnp.mean(yg * yg, axis=-1, keepdims=True) + NORM_EPS)
        mix_ref[m0:m0 + q, gc] = (yg * sng_ref[:, gc]).astype(BF16)


def _gmlp_chunk(r0, m0, proj_ref, u_ref, v_ref, mix_ref, lng_ref, lnb_ref, wsp_ref, bsp_ref):
    q = GMLP_CHUNK
    tot = jnp.zeros((q, 1), F32)
    for c0 in range(0, GMLP_WIDTH, 2 * LANES):
        cols = slice(c0, c0 + 2 * LANES)
        u_ref[:, cols] = _gelu(proj_ref[r0:r0 + q, E_U + c0:E_U + c0 + 2 * LANES])
        v = _gelu(proj_ref[r0:r0 + q, E_V + c0:E_V + c0 + 2 * LANES])
        v_ref[:, cols] = v
        tot = tot + jnp.sum(v, axis=-1, keepdims=True)
    mu = tot * (1.0 / GMLP_WIDTH)
    sq = jnp.zeros((q, 1), F32)
    for c0 in range(0, GMLP_WIDTH, 2 * LANES):
        dv = v_ref[:, c0:c0 + 2 * LANES] - mu
        sq = sq + jnp.sum(dv * dv, axis=-1, keepdims=True)
    inv = lax.rsqrt(sq * (1.0 / GMLP_WIDTH) + NORM_EPS)
    for g in range(GMLP_GROUPS):
        cols = slice(g * GMLP_GROUP_DIM, (g + 1) * GMLP_GROUP_DIM)
        vn = ((v_ref[:, cols] - mu) * inv * lng_ref[:, cols] + lnb_ref[:, cols]).astype(BF16)
        mixed = _dot(wsp_ref[g], vn) + bsp_ref[:, cols]
        mix_ref[m0:m0 + q, SSD_WIDTH + g * GMLP_GROUP_DIM:SSD_WIDTH + (g + 1) * GMLP_GROUP_DIM] = (
            u_ref[:, cols] * mixed).astype(BF16)


def _even_kernel(x_ref, ng_ref, wzx_ref, wuv_ref, wdt_ref, wout_ref, convw_ref, convb_ref, dtb_ref, a_ref, dskip_ref,
                 sng_ref, lng_ref, lnb_ref, wsp_ref, bsp_ref, expand_ref, tril_ref, o_ref,
                 proj_ref, mix_ref, xbc_ref, y_ref, u_ref, v_ref, state_ref):
    ts = x_ref.shape[1]
    pad = SUBLANES
    xbc_cols = slice(E_XBC, E_XBC + SSD_XBC)

    @pl.when(pl.program_id(1) == 0)
    def _():
        state_ref[...] = jnp.zeros(state_ref.shape, F32)
        proj_ref[0:pad, xbc_cols] = jnp.zeros((pad, SSD_XBC), F32)

    @pl.when(pl.program_id(1) > 0)
    def _():
        proj_ref[0:pad, xbc_cols] = proj_ref[ts:ts + pad, xbc_cols]

    xb = x_ref[0]
    h = _rmsnorm(xb, ng_ref[...]).astype(BF16)
    _project(h, wzx_ref, proj_ref, pad, ts, E_Z)
    _project(h, wuv_ref, proj_ref, pad, ts, E_U)
    _project(h, wdt_ref, proj_ref, pad, ts, E_DT)

    for c in range(ts // SSD_CHUNK):
        _ssd_chunk(pad + c * SSD_CHUNK, c * SSD_CHUNK, proj_ref, xbc_ref, y_ref, state_ref, mix_ref,
                   convw_ref, convb_ref, dtb_ref, a_ref, dskip_ref, sng_ref, expand_ref, tril_ref)
        _gmlp_chunk(pad + c * GMLP_CHUNK, c * GMLP_CHUNK, proj_ref, u_ref, v_ref, mix_ref,
                    lng_ref, lnb_ref, wsp_ref, bsp_ref)

    o_ref[0] = xb + _dot(mix_ref[...], wout_ref[...])


def _conf_chunk(r0, m0, proj_ref, glu_ref, cv_ref, mix_ref, convw_ref, convb_ref, lng_ref, lnb_ref):
    q = CONF_CHUNK
    g0 = CONF_HIST + m0
    win = CONF_HIST + q
    lead = CONF_HIST - (CONF_KERNEL - 1)
    tot = jnp.zeros((q, 1), F32)
    for c0 in range(0, CONF_WIDTH, CONF_TILE):
        cols = slice(c0, c0 + CONF_TILE)
        a = proj_ref[r0:r0 + q, O_GLU + 2 * c0:O_GLU + 2 * c0 + CONF_TILE]
        gate = proj_ref[r0:r0 + q, O_GLU + 2 * c0 + CONF_TILE:O_GLU + 2 * c0 + 2 * CONF_TILE]
        glu = a * jax.nn.sigmoid(gate)
        glu_ref[g0:g0 + q, cols] = glu
        window = jnp.concatenate([glu_ref[g0 - CONF_HIST:g0, cols], glu], axis=0)
        acc = jnp.broadcast_to(convb_ref[:, cols], (q, CONF_TILE))
        for res in range(SUBLANES):
            rolled = window if res == 0 else pltpu.roll(window, win - res, axis=0)
            for a8 in range(0, win - q + 1, SUBLANES):
                k = a8 + res - lead
                if 0 <= k < CONF_KERNEL:
                    acc = acc + rolled[a8:a8 + q, :] * convw_ref[k:k + 1, cols]
        cv_ref[:, cols] = acc
        tot = tot + jnp.sum(acc, axis=-1, keepdims=True)
    mu = tot * (1.0 / CONF_WIDTH)
    sq = jnp.zeros((q, 1), F32)
    for c0 in range(0, CONF_WIDTH, 2 * LANES):
        dv = cv_ref[:, c0:c0 + 2 * LANES] - mu
        sq = sq + jnp.sum(dv * dv, axis=-1, keepdims=True)
    inv = lax.rsqrt(sq * (1.0 / CONF_WIDTH) + NORM_EPS)
    for c0 in range(0, CONF_WIDTH, 2 * LANES):
        cols = slice(c0, c0 + 2 * LANES)
        hn = (cv_ref[:, cols] - mu) * inv * lng_ref[:, cols] + lnb_ref[:, cols]
        mix_ref[m0:m0 + q, cols] = _silu(hn).astype(BF16)


def _hgrn_block(ts, proj_ref, mix_ref, state_ref, lb_ref, hng_ref, tril_ref, lf_ref, kk_ref, bc_ref,
                qg_ref, kg_ref, qd_ref, kd_ref, cd_ref, iv_ref):
    c = HGRN_CHUNK
    lf_ref, kk_ref, bc_ref, qg_ref, kg_ref, qd_ref, kd_ref, cd_ref, iv_ref = _in_memory(
        lf_ref, kk_ref, bc_ref, qg_ref, kg_ref, qd_ref, kd_ref, cd_ref, iv_ref)
    heads = [(slice(hd * HGRN_EXPAND, (hd + 1) * HGRN_EXPAND),
              slice(hd * HGRN_HEAD_V, (hd + 1) * HGRN_HEAD_V)) for hd in range(HGRN_HEADS)]

    for kc, _ in heads:
        lb = lb_ref[:, kc]
        one_m_lb = 1.0 - lb
        fx = proj_ref[0:ts, O_F + kc.start:O_F + kc.stop]
        e = jnp.exp(-jnp.abs(fx))
        rcp = 1.0 / (1.0 + e)
        e_rcp = e * rcp
        pos = fx >= 0.0
        log_f = jnp.log(lb + one_m_lb * jnp.where(pos, rcp, e_rcp))
        kk_ref[:, kc] = one_m_lb * jnp.where(pos, e_rcp, rcp)
        hi, lo = _split_bf16(log_f, 2)
        lf_ref[:, kc] = hi
        lf_ref[:, HGRN_K + kc.start:HGRN_K + kc.stop] = lo

    for c0 in range(0, HGRN_K, PROJ_COLS):
        bc_ref[:, c0:c0 + PROJ_COLS] = (_dot(tril_ref[...], lf_ref[:, c0:c0 + PROJ_COLS])
                                        + _dot(tril_ref[...], lf_ref[:, HGRN_K + c0:HGRN_K + c0 + PROJ_COLS]))

    for ch in range(ts // c):
        rows = slice(ch * c, (ch + 1) * c)
        for kc, _ in heads:
            bcum = bc_ref[rows, kc]
            mid = bc_ref[ch * c + c // 2:ch * c + c // 2 + 1, kc]
            last = bc_ref[ch * c + c - 1:ch * c + c, kc]
            qv = proj_ref[rows, O_Q + kc.start:O_Q + kc.stop] * (HGRN_EXPAND ** -0.5)
            qg = qv * jnp.exp(bcum - mid)
            kg = kk_ref[rows, kc] * jnp.exp(mid - bcum)
            qg_ref[rows, kc] = qg.astype(BF16)
            kg_ref[rows, kc] = kg.astype(BF16)
            qd_ref[rows, kc] = (qg * jnp.exp(mid)).astype(BF16)
            kd_ref[rows, kc] = (kg * jnp.exp(last - mid)).astype(BF16)
            cd_ref[ch, :, kc] = jnp.exp(last)
    for _, vc in heads:
        iv_ref[:, vc] = proj_ref[0:ts, O_I + vc.start:O_I + vc.stop].astype(BF16)

    row = lax.broadcasted_iota(jnp.int32, (c, c), 0)
    col = lax.broadcasted_iota(jnp.int32, (c, c), 1)
    causal = row >= col

    for ch in range(ts // c):
        rows = slice(ch * c, (ch + 1) * c)
        for hd, (kc, vc) in enumerate(heads):
            att = jnp.where(causal, _dot_nt(qg_ref[rows, kc], kg_ref[rows, kc]), 0.0).astype(BF16)
            iv_b = iv_ref[rows, vc]
            st = state_ref[hd]
            o = _dot(att, iv_b) + _dot_nt(qd_ref[rows, kc], st.astype(BF16))
            state_ref[hd] = st * cd_ref[ch, :, kc] + _dot_tn(iv_b, kd_ref[rows, kc])
            o = o * lax.rsqrt(jnp.mean(o * o, axis=-1, keepdims=True) + NORM_EPS) * hng_ref[:, vc]
            gx = proj_ref[rows, O_G + vc.start:O_G + vc.stop]
            mix_ref[rows, CONF_WIDTH + vc.start:CONF_WIDTH + vc.stop] = (o * _silu(gx)).astype(BF16)


def _odd_kernel(x_ref, ng_ref, win_ref, wout_ref, convw_ref, convb_ref, lng_ref, lnb_ref, lb_ref,
                hng_ref, tril_ref, o_ref, proj_ref, mix_ref, glu_ref, cv_ref, state_ref,
                lf_ref, kk_ref, bc_ref, qg_ref, kg_ref, qd_ref, kd_ref, cd_ref, iv_ref):
    ts = x_ref.shape[1]

    @pl.when(pl.program_id(1) == 0)
    def _():
        state_ref[...] = jnp.zeros(state_ref.shape, F32)
        glu_ref[0:CONF_HIST, :] = jnp.zeros((CONF_HIST, CONF_WIDTH), F32)

    @pl.when(pl.program_id(1) > 0)
    def _():
        glu_ref[0:CONF_HIST, :] = glu_ref[ts:ts + CONF_HIST, :]

    xb = x_ref[0]
    h = _rmsnorm(xb, ng_ref[...]).astype(BF16)
    _project(h, win_ref, proj_ref, 0, ts)

    for c in range(ts // CONF_CHUNK):
        _conf_chunk(c * CONF_CHUNK, c * CONF_CHUNK, proj_ref, glu_ref, cv_ref, mix_ref,
                    convw_ref, convb_ref, lng_ref, lnb_ref)
    _hgrn_block(ts, proj_ref, mix_ref, state_ref, lb_ref, hng_ref, tril_ref, lf_ref, kk_ref, bc_ref,
                qg_ref, kg_ref, qd_ref, kd_ref, cd_ref, iv_ref)

    o_ref[0] = (xb + _dot(mix_ref[:, :CONF_WIDTH], wout_ref[:CONF_WIDTH, :])
                + _dot(mix_ref[:, CONF_WIDTH:], wout_ref[CONF_WIDTH:, :]))


def _ffn_kernel(x_ref, ng_ref, w1_ref, w2_ref, fg_ref, o_ref, hid_ref, *, final_norm):
    xb = x_ref[...]
    h = _rmsnorm(xb, ng_ref[...]).astype(BF16)
    for c0 in range(0, D_FF, PROJ_COLS):
        a = jnp.maximum(_dot(h, w1_ref[:, c0:c0 + PROJ_COLS]), 0.0)
        hid_ref[:, c0:c0 + PROJ_COLS] = (a * a).astype(BF16)
    y = xb + _dot(hid_ref[...], w2_ref[...])
    if final_norm:
        y = _rmsnorm(y, fg_ref[...])
    o_ref[...] = y


def _const_spec(shape):
    zeros = (0,) * len(shape)
    return pl.BlockSpec(shape, lambda *_: zeros, pipeline_mode=pl.Buffered(1))


def _mixer_call(body, x, consts, scratch, name):
    bsz, seq, d = x.shape
    ts = SEQ_BLOCK
    x_spec = pl.BlockSpec((1, ts, d), lambda b, s: (b, s, 0))
    return pl.pallas_call(
        body,
        grid=(bsz, seq // ts),
        in_specs=[x_spec] + [_const_spec(c.shape) for c in consts],
        out_specs=x_spec,
        out_shape=jax.ShapeDtypeStruct(x.shape, x.dtype),
        scratch_shapes=scratch,
        compiler_params=pltpu.CompilerParams(dimension_semantics=("arbitrary", "arbitrary"),
                                             vmem_limit_bytes=VMEM_LIMIT),
        name=name,
    )(x, *consts)


def _row(v):
    return v.reshape(1, -1).astype(F32)


def _lane_pad(v):
    return jnp.pad(_row(v), ((0, 0), (0, LANES - v.shape[-1])))


def _even_layer(x, ng, w_in, w_out, conv_w, conv_b, dt_bias, a_log, d_skip, ssd_ng, ln_g, ln_b, w_s, b_s):
    dt0 = SSD_WIDTH + SSD_XBC
    w_zx = w_in[:, :dt0].astype(BF16)
    w_uv = w_in[:, dt0 + SSD_HEADS:].astype(BF16)
    w_dt = jnp.pad(w_in[:, dt0:dt0 + SSD_HEADS], ((0, 0), (0, LANES - SSD_HEADS))).astype(BF16)
    tri = np.tril(np.ones((GMLP_CHUNK, GMLP_CHUNK), np.float32))
    w_sp = jnp.where(tri[None] > 0, w_s, 0.0).astype(BF16)
    b_sp = jnp.repeat(jnp.transpose(b_s).astype(F32), GMLP_GROUP_DIM, axis=1)
    expand = np.zeros((2 * LANES, SSD_WIDTH), np.float32)
    for part in range(2):
        for hd in range(SSD_HEADS):
            expand[part * LANES + hd, hd * SSD_HEAD_DIM:(hd + 1) * SSD_HEAD_DIM] = 1.0
    consts = [
        _row(ng), w_zx, w_uv, w_dt, w_out.astype(BF16), conv_w.astype(F32), _row(conv_b), _lane_pad(dt_bias),
        _lane_pad(-jnp.exp(a_log.astype(F32))), _row(jnp.repeat(d_skip.astype(F32), SSD_HEAD_DIM)),
        _row(ssd_ng), _row(ln_g), _row(ln_b), w_sp, b_sp,
        jnp.asarray(expand, BF16), jnp.asarray(tri, BF16),
    ]
    ts = SEQ_BLOCK
    scratch = [
        pltpu.VMEM((SUBLANES + ts, E_NIN), F32),
        pltpu.VMEM((ts, EVEN_MIX), BF16),
        pltpu.VMEM((SSD_CHUNK, SSD_XBC), F32),
        pltpu.VMEM((SSD_CHUNK, SSD_WIDTH), F32),
        pltpu.VMEM((GMLP_CHUNK, GMLP_WIDTH), F32),
        pltpu.VMEM((GMLP_CHUNK, GMLP_WIDTH), F32),
        pltpu.VMEM((SSD_GROUPS, SSD_STATE, GROUP_W), F32),
    ]
    return _mixer_call(_even_kernel, x, consts, scratch, "even_mixer")


def _odd_layer(x, ng, w_in, w_out, conv_w, conv_b, ln_g, ln_b, lb, hgrn_ng):
    ts = SEQ_BLOCK
    pieces = CONF_WIDTH // CONF_TILE
    glu_w = jnp.stack([w_in[:, :CONF_WIDTH].reshape(D_MODEL, pieces, CONF_TILE),
                       w_in[:, CONF_WIDTH:2 * CONF_WIDTH].reshape(D_MODEL, pieces, CONF_TILE)], axis=2)
    w_in_r = jnp.concatenate([glu_w.reshape(D_MODEL, 2 * CONF_WIDTH), w_in[:, 2 * CONF_WIDTH:]],
                             axis=1).astype(BF16)
    tri = np.kron(np.eye(ts // HGRN_CHUNK, dtype=np.float32),
                  np.tril(np.ones((HGRN_CHUNK, HGRN_CHUNK), np.float32)))
    consts = [
        _row(ng), w_in_r, w_out.astype(BF16), conv_w.astype(F32), _row(conv_b),
        _row(ln_g), _row(ln_b), _row(lb), _row(hgrn_ng), jnp.asarray(tri, BF16),
    ]
    scratch = [
        pltpu.VMEM((ts, O_NIN), F32),
        pltpu.VMEM((ts, ODD_MIX), BF16),
        pltpu.VMEM((CONF_HIST + ts, CONF_WIDTH), F32),
        pltpu.VMEM((CONF_CHUNK, CONF_WIDTH), F32),
        pltpu.VMEM((HGRN_HEADS, HGRN_HEAD_V, HGRN_EXPAND), F32),
        pltpu.VMEM((1, ts, 2 * HGRN_K), BF16),
        pltpu.VMEM((1, ts, HGRN_K), F32),
        pltpu.VMEM((1, ts, HGRN_K), F32),
        pltpu.VMEM((1, ts, HGRN_K), BF16),
        pltpu.VMEM((1, ts, HGRN_K), BF16),
        pltpu.VMEM((1, ts, HGRN_K), BF16),
        pltpu.VMEM((1, ts, HGRN_K), BF16),
        pltpu.VMEM((1, ts // HGRN_CHUNK, 1, HGRN_K), F32),
        pltpu.VMEM((1, ts, HGRN_V), BF16),
    ]
    return _mixer_call(_odd_kernel, x, consts, scratch, "odd_mixer")


def _ffn_layer(x2d, ng, w1, w2, final_g, final_norm):
    rows, d = x2d.shape
    tm = FFN_ROWS
    x_spec = pl.BlockSpec((tm, d), lambda i: (i, 0))
    consts = [_row(ng), w1.astype(BF16), w2.astype(BF16), _row(final_g)]
    return pl.pallas_call(
        functools.partial(_ffn_kernel, final_norm=final_norm),
        grid=(rows // tm,),
        in_specs=[x_spec] + [_const_spec(c.shape) for c in consts],
        out_specs=x_spec,
        out_shape=jax.ShapeDtypeStruct(x2d.shape, x2d.dtype),
        scratch_shapes=[pltpu.VMEM((tm, D_FF), BF16)],
        compiler_params=pltpu.CompilerParams(dimension_semantics=("arbitrary",),
                                             vmem_limit_bytes=VMEM_LIMIT),
        name="ffn",
    )(x2d, *consts)


def kernel(x, even_w_in, even_w_out, ssd_conv_w, ssd_conv_b, ssd_dt_bias, ssd_a_log, ssd_d, ssd_norm_g, gmlp_ln_g, gmlp_ln_b, gmlp_w_s, gmlp_b_s, odd_w_in, odd_w_out, conf_conv_w, conf_conv_b, conf_ln_g, conf_ln_b, hgrn_lb_logits, hgrn_norm_g, mix_norm_g, ffn_norm_g, ffn_w1, ffn_w2, final_norm_g):
    bsz, seq, d = x.shape
    depth = mix_norm_g.shape[0]
    assert d == D_MODEL and seq % SEQ_BLOCK == 0 and (bsz * seq) % FFN_ROWS == 0
    lb_all = jnp.cumsum(jax.nn.softmax(hgrn_lb_logits.astype(F32), axis=0), axis=0)
    lb_all = lb_all - lb_all[:1]
    for layer in range(depth):
        i = layer // 2
        if layer % 2 == 0:
            x = _even_layer(x, mix_norm_g[layer], even_w_in[i], even_w_out[i], ssd_conv_w[i], ssd_conv_b[i],
                            ssd_dt_bias[i], ssd_a_log[i], ssd_d[i], ssd_norm_g[i], gmlp_ln_g[i],
                            gmlp_ln_b[i], gmlp_w_s[i], gmlp_b_s[i])
        else:
            x = _odd_layer(x, mix_norm_g[layer], odd_w_in[i], odd_w_out[i], conf_conv_w[i], conf_conv_b[i],
                           conf_ln_g[i], conf_ln_b[i], lb_all[i], hgrn_norm_g[i])
        x = _ffn_layer(x.reshape(bsz * seq, d), ffn_norm_g[layer], ffn_w1[layer], ffn_w2[layer],
                       final_norm_g, layer == depth - 1).reshape(bsz, seq, d)
    return x
```

```python
import functools

import jax
import jax.numpy as jnp
import numpy as np
from jax import lax
from jax.experimental import pallas as pl
from jax.experimental.pallas import tpu as pltpu

F32 = jnp.float32
BF16 = jnp.bfloat16

D_MODEL = 1024
D_FF = 4 * D_MODEL
NORM_EPS = 1e-5

SSD_WIDTH = D_MODEL
SSD_HEAD_DIM = 64
SSD_HEADS = SSD_WIDTH // SSD_HEAD_DIM
SSD_GROUPS = 2
SSD_HPG = SSD_HEADS // SSD_GROUPS
SSD_STATE = 128
SSD_CONV = 4
SSD_CHUNK = 128
SSD_XBC = SSD_WIDTH + 2 * SSD_GROUPS * SSD_STATE
GROUP_W = SSD_HPG * SSD_HEAD_DIM

GMLP_WIDTH = D_MODEL
GMLP_GROUPS = 8
GMLP_GROUP_DIM = GMLP_WIDTH // GMLP_GROUPS
GMLP_CHUNK = 128

CONF_WIDTH = D_MODEL
CONF_KERNEL = 31
CONF_CHUNK = 128
CONF_HIST = 32

HGRN_HEADS = 8
HGRN_EXPAND = 128
HGRN_HEAD_V = D_MODEL // HGRN_HEADS
HGRN_K = HGRN_HEADS * HGRN_EXPAND
HGRN_V = HGRN_HEADS * HGRN_HEAD_V
HGRN_CHUNK = 64

LANES = 128
SUBLANES = 8

E_Z = 0
E_XBC = E_Z + SSD_WIDTH
E_U = E_XBC + SSD_XBC
E_V = E_U + GMLP_WIDTH
E_DT = E_V + GMLP_WIDTH
E_NIN = E_DT + LANES
EVEN_MIX = SSD_WIDTH + GMLP_WIDTH

CONF_TILE = LANES
O_GLU = 0
O_Q = O_GLU + 2 * CONF_WIDTH
O_F = O_Q + HGRN_K
O_I = O_F + HGRN_K
O_G = O_I + HGRN_V
O_NIN = O_G + HGRN_V
ODD_MIX = CONF_WIDTH + HGRN_V

SEQ_BLOCK = 512
FFN_ROWS = 1024
PROJ_COLS = 512
VMEM_LIMIT = 56 * 1024 * 1024


def _dot(a, b):
    return jnp.dot(a, b, preferred_element_type=F32)


def _dot_nt(a, b):
    return lax.dot_general(a, b, (((1,), (1,)), ((), ())), preferred_element_type=F32)


def _dot_tn(a, b):
    return lax.dot_general(a, b, (((0,), (0,)), ((), ())), preferred_element_type=F32)


def _rmsnorm(x, g):
    return x * lax.rsqrt(jnp.mean(x * x, axis=-1, keepdims=True) + NORM_EPS) * g


def _silu(x):
    return x * jax.nn.sigmoid(x)


def _gelu(x):
    return 0.5 * x * (1.0 + lax.erf(x * (2.0 ** -0.5)))


def _split_bf16(x, parts):
    out = []
    rem = x
    for _ in range(parts):
        p = rem.astype(BF16)
        out.append(p)
        rem = rem - p.astype(F32)
    return out


def _in_memory(*refs):
    z = jnp.minimum(pl.program_id(0), 0)
    return [r.at[z] for r in refs]


def _project(h, w_ref, proj_ref, row0, rows, col0=0):
    n_cols = w_ref.shape[1]
    for c0 in range(0, n_cols, PROJ_COLS):
        c1 = min(c0 + PROJ_COLS, n_cols)
        proj_ref[row0:row0 + rows, col0 + c0:col0 + c1] = _dot(h, w_ref[:, c0:c1])


def _ssd_chunk(r0, m0, proj_ref, xbc_ref, y_ref, state_ref, mix_ref, convw_ref, convb_ref, dtb_ref,
               a_ref, dskip_ref, sng_ref, expand_ref, tril_ref):
    q = SSD_CHUNK
    for c0 in range(0, SSD_XBC, 2 * LANES):
        cols = slice(c0, c0 + 2 * LANES)
        pcols = slice(E_XBC + c0, E_XBC + c0 + 2 * LANES)
        acc = jnp.broadcast_to(convb_ref[:, cols], (q, 2 * LANES))
        for k in range(SSD_CONV):
            acc = acc + proj_ref[pl.ds(r0 - (SSD_CONV - 1) + k, q), pcols] * convw_ref[k:k + 1, cols]
        xbc_ref[:, cols] = _silu(acc)

    dt = jax.nn.softplus(proj_ref[r0:r0 + q, E_DT:E_DT + LANES] + dtb_ref[...])
    da = dt * a_ref[...]
    cs = _dot(tril_ref[...], jnp.concatenate(_split_bf16(da, 3), axis=1))
    acs = cs[:, :LANES] + cs[:, LANES:2 * LANES] + cs[:, 2 * LANES:]
    acs_t = acs.T

    def expand(v):
        return _dot(jnp.concatenate(_split_bf16(v, 2), axis=1), expand_ref[...])

    dt_x = expand(dt)
    acs_x = expand(acs)

    row = lax.broadcasted_iota(jnp.int32, (q, q), 0)
    col = lax.broadcasted_iota(jnp.int32, (q, q), 1)
    causal = row >= col
    lane = lax.broadcasted_iota(jnp.int32, (q, LANES), 1)
    first_half = lane < SSD_HEAD_DIM

    for g in range(SSD_GROUPS):
        gc = slice(g * GROUP_W, (g + 1) * GROUP_W)
        acs_g = acs_x[:, gc]
        last_g = acs_g[q - 1:q, :]
        xs_g = xbc_ref[:, gc]
        xdt_g = xs_g * dt_x[:, gc]
        xdt_b = xdt_g.astype(BF16)
        xw_b = (xdt_g * jnp.exp(last_g - acs_g)).astype(BF16)
        bm_b = xbc_ref[:, SSD_WIDTH + g * SSD_STATE:SSD_WIDTH + (g + 1) * SSD_STATE].astype(BF16)
        cm0 = SSD_WIDTH + SSD_GROUPS * SSD_STATE + g * SSD_STATE
        cm_b = xbc_ref[:, cm0:cm0 + SSD_STATE].astype(BF16)
        st = state_ref[g]
        y_off = _dot(cm_b, st.astype(BF16)) * jnp.exp(acs_g)
        state_ref[g] = st * jnp.exp(last_g) + _dot_tn(bm_b, xw_b)
        cb = _dot_nt(cm_b, bm_b)
        for jp in range(SSD_HPG // 2):
            pc = slice(jp * LANES, (jp + 1) * LANES)
            x_pair = xdt_b[:, pc]
            yd = []
            for jj in range(2):
                hd = g * SSD_HPG + 2 * jp + jj
                seg = acs[:, hd:hd + 1] - acs_t[hd:hd + 1, :]
                m = (cb * jnp.where(causal, jnp.exp(seg), 0.0)).astype(BF16)
                yd.append(_dot(m, x_pair))
            y_diag = jnp.where(first_half, yd[0], yd[1])
            yc = slice(g * GROUP_W + jp * LANES, g * GROUP_W + (jp + 1) * LANES)
            y_ref[:, yc] = y_diag + y_off[:, pc] + dskip_ref[:, yc] * xs_g[:, pc]

    for g in range(SSD_GROUPS):
        gc = slice(g * GROUP_W, (g + 1) * GROUP_W)
        yg = y_ref[:, gc] * _silu(proj_ref[r0:r0 + q, E_Z + g * GROUP_W:E_Z + (g + 1) * GROUP_W])
        yg = yg * lax.rsqrt(jnp.mean(yg * yg, axis=-1, keepdims=True) + NORM_EPS)
        mix_ref[m0:m0 + q, gc] = (yg * sng_ref[:, gc]).astype(BF16)


def _gmlp_chunk(r0, m0, proj_ref, u_ref, v_ref, mix_ref, lng_ref, lnb_ref, wsp_ref, bsp_ref):
    q = GMLP_CHUNK
    tot = jnp.zeros((q, 1), F32)
    for c0 in range(0, GMLP_WIDTH, 2 * LANES):
        cols = slice(c0, c0 + 2 * LANES)
        u_ref[:, cols] = _gelu(proj_ref[r0:r0 + q, E_U + c0:E_U + c0 + 2 * LANES])
        v = _gelu(proj_ref[r0:r0 + q, E_V + c0:E_V + c0 + 2 * LANES])
        v_ref[:, cols] = v
        tot = tot + jnp.sum(v, axis=-1, keepdims=True)
    mu = tot * (1.0 / GMLP_WIDTH)
    sq = jnp.zeros((q, 1), F32)
    for c0 in range(0, GMLP_WIDTH, 2 * LANES):
        dv = v_ref[:, c0:c0 + 2 * LANES] - mu
        sq = sq + jnp.sum(dv * dv, axis=-1, keepdims=True)
    inv = lax.rsqrt(sq * (1.0 / GMLP_WIDTH) + NORM_EPS)
    for g in range(GMLP_GROUPS):
        cols = slice(g * GMLP_GROUP_DIM, (g + 1) * GMLP_GROUP_DIM)
        vn = ((v_ref[:, cols] - mu) * inv * lng_ref[:, cols] + lnb_ref[:, cols]).astype(BF16)
        mixed = _dot(wsp_ref[g], vn) + bsp_ref[:, cols]
        mix_ref[m0:m0 + q, SSD_WIDTH + g * GMLP_GROUP_DIM:SSD_WIDTH + (g + 1) * GMLP_GROUP_DIM] = (
            u_ref[:, cols] * mixed).astype(BF16)


def _even_kernel(x_ref, ng_ref, wzx_ref, wuv_ref, wdt_ref, wout_ref, convw_ref, convb_ref, dtb_ref, a_ref, dskip_ref,
                 sng_ref, lng_ref, lnb_ref, wsp_ref, bsp_ref, expand_ref, tril_ref, o_ref,
                 proj_ref, mix_ref, xbc_ref, y_ref, u_ref, v_ref, state_ref):
    ts = x_ref.shape[1]
    pad = SUBLANES
    xbc_cols = slice(E_XBC, E_XBC + SSD_XBC)

    @pl.when(pl.program_id(1) == 0)
    def _():
        state_ref[...] = jnp.zeros(state_ref.shape, F32)
        proj_ref[0:pad, xbc_cols] = jnp.zeros((pad, SSD_XBC), F32)

    @pl.when(pl.program_id(1) > 0)
    def _():
        proj_ref[0:pad, xbc_cols] = proj_ref[ts:ts + pad, xbc_cols]

    xb = x_ref[0]
    h = _rmsnorm(xb, ng_ref[...]).astype(BF16)
    _project(h, wzx_ref, proj_ref, pad, ts, E_Z)
    _project(h, wuv_ref, proj_ref, pad, ts, E_U)
    _project(h, wdt_ref, proj_ref, pad, ts, E_DT)

    for c in range(ts // SSD_CHUNK):
        _ssd_chunk(pad + c * SSD_CHUNK, c * SSD_CHUNK, proj_ref, xbc_ref, y_ref, state_ref, mix_ref,
                   convw_ref, convb_ref, dtb_ref, a_ref, dskip_ref, sng_ref, expand_ref, tril_ref)
        _gmlp_chunk(pad + c * GMLP_CHUNK, c * GMLP_CHUNK, proj_ref, u_ref, v_ref, mix_ref,
                    lng_ref, lnb_ref, wsp_ref, bsp_ref)

    o_ref[0] = xb + _dot(mix_ref[...], wout_ref[...])


def _conf_chunk(r0, m0, proj_ref, glu_ref, cv_ref, mix_ref, convw_ref, convb_ref, lng_ref, lnb_ref):
    q = CONF_CHUNK
    g0 = CONF_HIST + m0
    win = CONF_HIST + q
    lead = CONF_HIST - (CONF_KERNEL - 1)
    tot = jnp.zeros((q, 1), F32)
    for c0 in range(0, CONF_WIDTH, CONF_TILE):
        cols = slice(c0, c0 + CONF_TILE)
        a = proj_ref[r0:r0 + q, O_GLU + 2 * c0:O_GLU + 2 * c0 + CONF_TILE]
        gate = proj_ref[r0:r0 + q, O_GLU + 2 * c0 + CONF_TILE:O_GLU + 2 * c0 + 2 * CONF_TILE]
        glu = a * jax.nn.sigmoid(gate)
        glu_ref[g0:g0 + q, cols] = glu
        window = jnp.concatenate([glu_ref[g0 - CONF_HIST:g0, cols], glu], axis=0)
        acc = jnp.broadcast_to(convb_ref[:, cols], (q, CONF_TILE))
        for res in range(SUBLANES):
            rolled = window if res == 0 else pltpu.roll(window, win - res, axis=0)
            for a8 in range(0, win - q + 1, SUBLANES):
                k = a8 + res - lead
                if 0 <= k < CONF_KERNEL:
                    acc = acc + rolled[a8:a8 + q, :] * convw_ref[k:k + 1, cols]
        cv_ref[:, cols] = acc
        tot = tot + jnp.sum(acc, axis=-1, keepdims=True)
    mu = tot * (1.0 / CONF_WIDTH)
    sq = jnp.zeros((q, 1), F32)
    for c0 in range(0, CONF_WIDTH, 2 * LANES):
        dv = cv_ref[:, c0:c0 + 2 * LANES] - mu
        sq = sq + jnp.sum(dv * dv, axis=-1, keepdims=True)
    inv = lax.rsqrt(sq * (1.0 / CONF_WIDTH) + NORM_EPS)
    for c0 in range(0, CONF_WIDTH, 2 * LANES):
        cols = slice(c0, c0 + 2 * LANES)
        hn = (cv_ref[:, cols] - mu) * inv * lng_ref[:, cols] + lnb_ref[:, cols]
        mix_ref[m0:m0 + q, cols] = _silu(hn).astype(BF16)


def _hgrn_block(ts, proj_ref, mix_ref, state_ref, lb_ref, hng_ref, tril_ref, lf_ref, kk_ref, bc_ref,
                qg_ref, kg_ref, qd_ref, kd_ref, cd_ref, iv_ref):
    c = HGRN_CHUNK
    lf_ref, kk_ref, bc_ref, qg_ref, kg_ref, qd_ref, kd_ref, cd_ref, iv_ref = _in_memory(
        lf_ref, kk_ref, bc_ref, qg_ref, kg_ref, qd_ref, kd_ref, cd_ref, iv_ref)
    heads = [(slice(hd * HGRN_EXPAND, (hd + 1) * HGRN_EXPAND),
              slice(hd * HGRN_HEAD_V, (hd + 1) * HGRN_HEAD_V)) for hd in range(HGRN_HEADS)]

    for kc, _ in heads:
        lb = lb_ref[:, kc]
        one_m_lb = 1.0 - lb
        fx = proj_ref[0:ts, O_F + kc.start:O_F + kc.stop]
        e = jnp.exp(-jnp.abs(fx))
        rcp = 1.0 / (1.0 + e)
        e_rcp = e * rcp
        pos = fx >= 0.0
        log_f = jnp.log(lb + one_m_lb * jnp.where(pos, rcp, e_rcp))
        kk_ref[:, kc] = one_m_lb * jnp.where(pos, e_rcp, rcp)
        hi, lo = _split_bf16(log_f, 2)
        lf_ref[:, kc] = hi
        lf_ref[:, HGRN_K + kc.start:HGRN_K + kc.stop] = lo

    for c0 in range(0, HGRN_K, PROJ_COLS):
        bc_ref[:, c0:c0 + PROJ_COLS] = (_dot(tril_ref[...], lf_ref[:, c0:c0 + PROJ_COLS])
                                        + _dot(tril_ref[...], lf_ref[:, HGRN_K + c0:HGRN_K + c0 + PROJ_COLS]))

    for ch in range(ts // c):
        rows = slice(ch * c, (ch + 1) * c)
        for kc, _ in heads:
            bcum = bc_ref[rows, kc]
            mid = bc_ref[ch * c + c // 2:ch * c + c // 2 + 1, kc]
            last = bc_ref[ch * c + c - 1:ch * c + c, kc]
            qv = proj_ref[rows, O_Q + kc.start:O_Q + kc.stop] * (HGRN_EXPAND ** -0.5)
            qg = qv * jnp.exp(bcum - mid)
            kg = kk_ref[rows, kc] * jnp.exp(mid - bcum)
            qg_ref[rows, kc] = qg.astype(BF16)
            kg_ref[rows, kc] = kg.astype(BF16)
            qd_ref[rows, kc] = (qg * jnp.exp(mid)).astype(BF16)
            kd_ref[rows, kc] = (kg * jnp.exp(last - mid)).astype(BF16)
            cd_ref[ch, :, kc] = jnp.exp(last)
    for _, vc in heads:
        iv_ref[:, vc] = proj_ref[0:ts, O_I + vc.start:O_I + vc.stop].astype(BF16)

    row = lax.broadcasted_iota(jnp.int32, (c, c), 0)
    col = lax.broadcasted_iota(jnp.int32, (c, c), 1)
    causal = row >= col

    for ch in range(ts // c):
        rows = slice(ch * c, (ch + 1) * c)
        for hd, (kc, vc) in enumerate(heads):
            att = jnp.where(causal, _dot_nt(qg_ref[rows, kc], kg_ref[rows, kc]), 0.0).astype(BF16)
            iv_b = iv_ref[rows, vc]
            st = state_ref[hd]
            o = _dot(att, iv_b) + _dot_nt(qd_ref[rows, kc], st.astype(BF16))
            state_ref[hd] = st * cd_ref[ch, :, kc] + _dot_tn(iv_b, kd_ref[rows, kc])
            o = o * lax.rsqrt(jnp.mean(o * o, axis=-1, keepdims=True) + NORM_EPS) * hng_ref[:, vc]
            gx = proj_ref[rows, O_G + vc.start:O_G + vc.stop]
            mix_ref[rows, CONF_WIDTH + vc.start:CONF_WIDTH + vc.stop] = (o * _silu(gx)).astype(BF16)


def _odd_kernel(x_ref, ng_ref, wglu_ref, wrest_ref, wout_ref, convw_ref, convb_ref, lng_ref, lnb_ref, lb_ref,
                hng_ref, tril_ref, o_ref, proj_ref, mix_ref, glu_ref, cv_ref, state_ref,
                lf_ref, kk_ref, bc_ref, qg_ref, kg_ref, qd_ref, kd_ref, cd_ref, iv_ref):
    ts = x_ref.shape[1]

    @pl.when(pl.program_id(1) == 0)
    def _():
        state_ref[...] = jnp.zeros(state_ref.shape, F32)
        glu_ref[0:CONF_HIST, :] = jnp.zeros((CONF_HIST, CONF_WIDTH), F32)

    @pl.when(pl.program_id(1) > 0)
    def _():
        glu_ref[0:CONF_HIST, :] = glu_ref[ts:ts + CONF_HIST, :]

    xb = x_ref[0]
    h = _rmsnorm(xb, ng_ref[...]).astype(BF16)
    _project(h, wglu_ref, proj_ref, 0, ts, O_GLU)
    _project(h, wrest_ref, proj_ref, 0, ts, O_Q)

    for c in range(ts // CONF_CHUNK):
        _conf_chunk(c * CONF_CHUNK, c * CONF_CHUNK, proj_ref, glu_ref, cv_ref, mix_ref,
                    convw_ref, convb_ref, lng_ref, lnb_ref)
    _hgrn_block(ts, proj_ref, mix_ref, state_ref, lb_ref, hng_ref, tril_ref, lf_ref, kk_ref, bc_ref,
                qg_ref, kg_ref, qd_ref, kd_ref, cd_ref, iv_ref)

    o_ref[0] = (xb + _dot(mix_ref[:, :CONF_WIDTH], wout_ref[:CONF_WIDTH, :])
                + _dot(mix_ref[:, CONF_WIDTH:], wout_ref[CONF_WIDTH:, :]))


def _ffn_kernel(x_ref, ng_ref, w1_ref, w2_ref, fg_ref, o_ref, hid_ref, *, final_norm):
    xb = x_ref[...]
    h = _rmsnorm(xb, ng_ref[...]).astype(BF16)
    for c0 in range(0, D_FF, PROJ_COLS):
        a = jnp.maximum(_dot(h, w1_ref[:, c0:c0 + PROJ_COLS]), 0.0)
        hid_ref[:, c0:c0 + PROJ_COLS] = (a * a).astype(BF16)
    y = xb + _dot(hid_ref[...], w2_ref[...])
    if final_norm:
        y = _rmsnorm(y, fg_ref[...])
    o_ref[...] = y


def _const_spec(shape):
    zeros = (0,) * len(shape)
    return pl.BlockSpec(shape, lambda *_: zeros, pipeline_mode=pl.Buffered(1))


def _mixer_call(body, x, consts, scratch, name):
    bsz, seq, d = x.shape
    ts = SEQ_BLOCK
    x_spec = pl.BlockSpec((1, ts, d), lambda b, s: (b, s, 0))
    return pl.pallas_call(
        body,
        grid=(bsz, seq // ts),
        in_specs=[x_spec] + [_const_spec(c.shape) for c in consts],
        out_specs=x_spec,
        out_shape=jax.ShapeDtypeStruct(x.shape, x.dtype),
        scratch_shapes=scratch,
        compiler_params=pltpu.CompilerParams(dimension_semantics=("arbitrary", "arbitrary"),
                                             vmem_limit_bytes=VMEM_LIMIT),
        name=name,
    )(x, *consts)


def _row(v):
    return v.reshape(1, -1).astype(F32)


def _lane_pad(v):
    return jnp.pad(_row(v), ((0, 0), (0, LANES - v.shape[-1])))


def _even_layer(x, ng, w_in, w_out, conv_w, conv_b, dt_bias, a_log, d_skip, ssd_ng, ln_g, ln_b, w_s, b_s):
    dt0 = SSD_WIDTH + SSD_XBC
    w_zx = w_in[:, :dt0].astype(BF16)
    w_uv = w_in[:, dt0 + SSD_HEADS:].astype(BF16)
    w_dt = jnp.pad(w_in[:, dt0:dt0 + SSD_HEADS], ((0, 0), (0, LANES - SSD_HEADS))).astype(BF16)
    tri = np.tril(np.ones((GMLP_CHUNK, GMLP_CHUNK), np.float32))
    w_sp = jnp.where(tri[None] > 0, w_s, 0.0).astype(BF16)
    b_sp = jnp.repeat(jnp.transpose(b_s).astype(F32), GMLP_GROUP_DIM, axis=1)
    expand = np.zeros((2 * LANES, SSD_WIDTH), np.float32)
    for part in range(2):
        for hd in range(SSD_HEADS):
            expand[part * LANES + hd, hd * SSD_HEAD_DIM:(hd + 1) * SSD_HEAD_DIM] = 1.0
    consts = [
        _row(ng), w_zx, w_uv, w_dt, w_out.astype(BF16), conv_w.astype(F32), _row(conv_b), _lane_pad(dt_bias),
        _lane_pad(-jnp.exp(a_log.astype(F32))), _row(jnp.repeat(d_skip.astype(F32), SSD_HEAD_DIM)),
        _row(ssd_ng), _row(ln_g), _row(ln_b), w_sp, b_sp,
        jnp.asarray(expand, BF16), jnp.asarray(tri, BF16),
    ]
    ts = SEQ_BLOCK
    scratch = [
        pltpu.VMEM((SUBLANES + ts, E_NIN), F32),
        pltpu.VMEM((ts, EVEN_MIX), BF16),
        pltpu.VMEM((SSD_CHUNK, SSD_XBC), F32),
        pltpu.VMEM((SSD_CHUNK, SSD_WIDTH), F32),
        pltpu.VMEM((GMLP_CHUNK, GMLP_WIDTH), F32),
        pltpu.VMEM((GMLP_CHUNK, GMLP_WIDTH), F32),
        pltpu.VMEM((SSD_GROUPS, SSD_STATE, GROUP_W), F32),
    ]
    return _mixer_call(_even_kernel, x, consts, scratch, "even_mixer")


def _odd_layer(x, ng, w_in, w_out, conv_w, conv_b, ln_g, ln_b, lb, hgrn_ng):
    ts = SEQ_BLOCK
    pieces = CONF_WIDTH // CONF_TILE
    glu_w = jnp.stack([w_in[:, :CONF_WIDTH].reshape(D_MODEL, pieces, CONF_TILE),
                       w_in[:, CONF_WIDTH:2 * CONF_WIDTH].reshape(D_MODEL, pieces, CONF_TILE)], axis=2)
    w_glu = glu_w.reshape(D_MODEL, 2 * CONF_WIDTH).astype(BF16)
    w_rest = w_in[:, 2 * CONF_WIDTH:].astype(BF16)
    tri = np.kron(np.eye(ts // HGRN_CHUNK, dtype=np.float32),
                  np.tril(np.ones((HGRN_CHUNK, HGRN_CHUNK), np.float32)))
    consts = [
        _row(ng), w_glu, w_rest, w_out.astype(BF16), conv_w.astype(F32), _row(conv_b),
        _row(ln_g), _row(ln_b), _row(lb), _row(hgrn_ng), jnp.asarray(tri, BF16),
    ]
    scratch = [
        pltpu.VMEM((ts, O_NIN), F32),
        pltpu.VMEM((ts, ODD_MIX), BF16),
        pltpu.VMEM((CONF_HIST + ts, CONF_WIDTH), F32),
        pltpu.VMEM((CONF_CHUNK, CONF_WIDTH), F32),
        pltpu.VMEM((HGRN_HEADS, HGRN_HEAD_V, HGRN_EXPAND), F32),
        pltpu.VMEM((1, ts, 2 * HGRN_K), BF16),
        pltpu.VMEM((1, ts, HGRN_K), F32),
        pltpu.VMEM((1, ts, HGRN_K), F32),
        pltpu.VMEM((1, ts, HGRN_K), BF16),
        pltpu.VMEM((1, ts, HGRN_K), BF16),
        pltpu.VMEM((1, ts, HGRN_K), BF16),
        pltpu.VMEM((1, ts, HGRN_K), BF16),
        pltpu.VMEM((1, ts // HGRN_CHUNK, 1, HGRN_K), F32),
        pltpu.VMEM((1, ts, HGRN_V), BF16),
    ]
    return _mixer_call(_odd_kernel, x, consts, scratch, "odd_mixer")


def _ffn_layer(x2d, ng, w1, w2, final_g, final_norm):
    rows, d = x2d.shape
    tm = FFN_ROWS
    x_spec = pl.BlockSpec((tm, d), lambda i: (i, 0))
    consts = [_row(ng), w1.astype(BF16), w2.astype(BF16), _row(final_g)]
    return pl.pallas_call(
        functools.partial(_ffn_kernel, final_norm=final_norm),
        grid=(rows // tm,),
        in_specs=[x_spec] + [_const_spec(c.shape) for c in consts],
        out_specs=x_spec,
        out_shape=jax.ShapeDtypeStruct(x2d.shape, x2d.dtype),
        scratch_shapes=[pltpu.VMEM((tm, D_FF), BF16)],
        compiler_params=pltpu.CompilerParams(dimension_semantics=("arbitrary",),
                                             vmem_limit_bytes=VMEM_LIMIT),
        name="ffn",
    )(x2d, *consts)


def kernel(x, even_w_in, even_w_out, ssd_conv_w, ssd_conv_b, ssd_dt_bias, ssd_a_log, ssd_d, ssd_norm_g, gmlp_ln_g, gmlp_ln_b, gmlp_w_s, gmlp_b_s, odd_w_in, odd_w_out, conf_conv_w, conf_conv_b, conf_ln_g, conf_ln_b, hgrn_lb_logits, hgrn_norm_g, mix_norm_g, ffn_norm_g, ffn_w1, ffn_w2, final_norm_g):
    bsz, seq, d = x.shape
    depth = mix_norm_g.shape[0]
    assert d == D_MODEL and seq % SEQ_BLOCK == 0 and (bsz * seq) % FFN_ROWS == 0
    lb_all = jnp.cumsum(jax.nn.softmax(hgrn_lb_logits.astype(F32), axis=0), axis=0)
    lb_all = lb_all - lb_all[:1]
    for layer in range(depth):
        i = layer // 2
        if layer % 2 == 0:
            x = _even_layer(x, mix_norm_g[layer], even_w_in[i], even_w_out[i], ssd_conv_w[i], ssd_conv_b[i],
                            ssd_dt_bias[i], ssd_a_log[i], ssd_d[i], ssd_norm_g[i], gmlp_ln_g[i],
                            gmlp_ln_b[i], gmlp_w_s[i], gmlp_b_s[i])
        else:
            x = _odd_layer(x, mix_norm_g[layer], odd_w_in[i], odd_w_out[i], conf_conv_w[i], conf_conv_b[i],
                           conf_ln_g[i], conf_ln_b[i], lb_all[i], hgrn_norm_g[i])
        x = _ffn_layer(x.reshape(bsz * seq, d), ffn_norm_g[layer], ffn_w1[layer], ffn_w2[layer],
                       final_norm_g, layer == depth - 1).reshape(bsz, seq, d)
    return x
```
